```python
import jax
import jax.numpy as jnp
from jax import lax
import numpy as np


D_MODEL = 1024
BATCH = 32
SEQ = 2048
DEPTH = 4

N_META = 16
CHUNK = 64
D_MIX = D_MODEL
MLA_HEADS = 8
QK_NOPE = 64
QK_ROPE = 32
V_HEAD = 64
Q_LORA = 768
KV_LORA = 256
ROPE_THETA = 10000.0
ATTN_BLOCK = 128
MLA_WIDTH = MLA_HEADS * V_HEAD
POOL_GROUPS = 4
POOL_GROUP_DIM = 64
POOL_WINDOWS = (2, 4, 8, 16)
POOL_WIDTH = POOL_GROUPS * POOL_GROUP_DIM
MLSTM_HEADS = 4
MLSTM_HEAD_DIM = 64
MLSTM_WIDTH = MLSTM_HEADS * MLSTM_HEAD_DIM
CONV_WIDTH = 4
MLSTM_BLOCK = 16
D_FF = 2816
N_EXPERTS = 8
TOP_K = 2
D_FF_EXPERT = 3584
MOE_BLOCK = 256
EPS = 1e-6
NEG_INF = -1e30
IN_SIZES = (Q_LORA, KV_LORA, QK_ROPE, POOL_WIDTH, 2 * MLSTM_WIDTH, MLSTM_WIDTH, MLSTM_WIDTH, 2 * MLSTM_HEADS)
D_IN = Q_LORA + KV_LORA + QK_ROPE + POOL_WIDTH + 4 * MLSTM_WIDTH + 2 * MLSTM_HEADS

kernel_name = "hybrid_meta_mla_pool_mlstm_moe_trunk"


def _rmsnorm(x, g):
    xf = x.astype(jnp.float32)
    y = xf * lax.rsqrt(jnp.mean(xf * xf, axis=-1, keepdims=True) + EPS)
    return (y * g.astype(jnp.float32)).astype(x.dtype)


def _rope(x, cos, sin):
    half = x.shape[-1] // 2
    x1, x2 = x[..., :half], x[..., half:]
    cos = cos.astype(x.dtype)
    sin = sin.astype(x.dtype)
    return jnp.concatenate([x1 * cos - x2 * sin, x2 * cos + x1 * sin], axis=-1)


def _mla_attention(qn, qr, kn, kr, v, cid):
    bsz, L, H, _ = qn.shape
    scale = (QK_NOPE + QK_ROPE) ** -0.5
    outs = []
    for q0 in range(0, L, ATTN_BLOCK):
        q1 = min(q0 + ATTN_BLOCK, L)
        kend = min(L, N_META + ((q1 - 1 - N_META) // CHUNK + 1) * CHUNK)
        s = (jnp.einsum('bqhd,bkhd->bhqk', qn[:, q0:q1], kn[:, :kend])
             + jnp.einsum('bqhr,bkr->bhqk', qr[:, q0:q1], kr[:, :kend]))
        s = s.astype(jnp.float32) * scale
        mask = cid[None, :kend] <= cid[q0:q1, None]
        p = jax.nn.softmax(jnp.where(mask, s, NEG_INF), axis=-1).astype(v.dtype)
        outs.append(jnp.einsum('bhqk,bkhd->bqhd', p, v[:, :kend]))
    return jnp.concatenate(outs, axis=1).reshape(bsz, L, H * V_HEAD)


def _multiscale_pool(xp, pool_w, pool_scale):
    bsz, L, _ = xp.shape
    xf = xp.astype(jnp.float32).reshape(bsz, L, POOL_GROUPS, POOL_GROUP_DIM)
    cs = jnp.cumsum(xf, axis=1)
    count = jnp.arange(1, L + 1, dtype=jnp.float32)
    diffs = []
    for g, w in enumerate(POOL_WINDOWS):
        c = cs[:, :, g]
        lag = jnp.pad(c[:, :L - w], ((0, 0), (w, 0), (0, 0)))
        mean = (c - lag) / jnp.minimum(count, float(w))[None, :, None]
        diffs.append(mean - xf[:, :, g])
    d = jnp.stack(diffs, axis=2)
    y = jnp.einsum('blgc,gcd->blgd', d, pool_w.astype(jnp.float32))
    return (y.reshape(bsz, L, POOL_WIDTH) * pool_scale.astype(jnp.float32)).astype(xp.dtype)


def _causal_depthwise_conv(x, w, b):
    C = x.shape[-1]
    y = lax.conv_general_dilated(x, w[:, None, :].astype(x.dtype), window_strides=(1,),
                                 padding=[(CONV_WIDTH - 1, 0)],
                                 dimension_numbers=('NWC', 'WIO', 'NWC'),
                                 feature_group_count=C)
    return y + b.astype(x.dtype)


def _mlstm(q, k, v, ig, fg):
    bsz, L, H, dh = q.shape
    T = MLSTM_BLOCK
    NC = L // T

    def chunks(a):
        return a.reshape(bsz, NC, T, H, -1).transpose(1, 0, 3, 2, 4)

    def chunks_g(a):
        return a.reshape(bsz, NC, T, H).transpose(1, 0, 3, 2)

    li = chunks_g(ig)
    lf = chunks_g(jax.nn.log_sigmoid(fg))
    tri = jnp.tril(jnp.ones((T, T), dtype=bool))

    def step(carry, inp):
        C, n, m = carry
        qc, kc, vc, lic, lfc = inp
        b = jnp.cumsum(lfc, axis=-1)
        D = jnp.where(tri, b[..., :, None] - b[..., None, :] + lic[..., None, :], NEG_INF)
        m_inter = b + m[..., None]
        m_t = jnp.maximum(m_inter, jnp.max(D, axis=-1))
        s = jnp.einsum('bhtd,bhsd->bhts', qc, kc) * jnp.exp(D - m_t[..., None])
        decay = jnp.exp(m_inter - m_t)
        num = (jnp.einsum('bhts,bhsd->bhtd', s, vc)
               + decay[..., None] * jnp.einsum('bhvk,bhtk->bhtv', C, qc))
        den = jnp.sum(s, axis=-1) + decay * jnp.einsum('bhk,bhtk->bht', n, qc)
        hc = num / jnp.maximum(jnp.abs(den), jnp.exp(-m_t))[..., None]
        g = b[..., -1:] - b + lic
        m_new = jnp.maximum(b[..., -1] + m, jnp.max(g, axis=-1))
        wg = jnp.exp(g - m_new[..., None])
        carry_decay = jnp.exp(b[..., -1] + m - m_new)
        C_new = carry_decay[..., None, None] * C + jnp.einsum('bhsv,bhsk->bhvk', vc * wg[..., None], kc)
        n_new = carry_decay[..., None] * n + jnp.einsum('bhs,bhsk->bhk', wg, kc)
        return (C_new, n_new, m_new), hc

    init = (jnp.zeros((bsz, H, dh, dh), jnp.float32),
            jnp.zeros((bsz, H, dh), jnp.float32),
            jnp.zeros((bsz, H), jnp.float32))
    _, hs = lax.scan(step, init, (chunks(q), chunks(k), chunks(v), li, lf))
    return hs.transpose(1, 0, 3, 2, 4).reshape(bsz, L, H, dh)


def _mixer(xn, w_in, q_norm, w_q_up, kv_norm, w_kv_up, pool_w, pool_scale,
           conv_w, conv_b, gate_bias, mlstm_norm, w_out, cos, sin, cid):
    bsz, L, _ = xn.shape
    z = xn @ w_in
    pts = np.cumsum(np.array(IN_SIZES))[:-1].tolist()
    cq, ckv, kr, xp, mqk, mv, mo, mg = jnp.split(z, pts, axis=-1)

    q = (_rmsnorm(cq, q_norm) @ w_q_up).reshape(bsz, L, MLA_HEADS, QK_NOPE + QK_ROPE)
    qn, qr = q[..., :QK_NOPE], _rope(q[..., QK_NOPE:], cos[:, None, :], sin[:, None, :])
    kv = (_rmsnorm(ckv, kv_norm) @ w_kv_up).reshape(bsz, L, MLA_HEADS, QK_NOPE + V_HEAD)
    kn, v = kv[..., :QK_NOPE], kv[..., QK_NOPE:]
    kr = _rope(kr, cos, sin)
    y_attn = _mla_attention(qn, qr, kn, kr, v, cid)

    y_pool = _multiscale_pool(xp, pool_w, pool_scale)

    qk = jax.nn.silu(_causal_depthwise_conv(mqk, conv_w, conv_b))
    mq, mk = qk[..., :MLSTM_WIDTH], qk[..., MLSTM_WIDTH:]
    gates = (mg + gate_bias).astype(jnp.float32)
    ig, fg = gates[..., :MLSTM_HEADS], gates[..., MLSTM_HEADS:]
    shp = (bsz, L, MLSTM_HEADS, MLSTM_HEAD_DIM)
    hm = _mlstm(mq.astype(jnp.float32).reshape(shp),
                mk.astype(jnp.float32).reshape(shp) * (MLSTM_HEAD_DIM ** -0.5),
                mv.astype(jnp.float32).reshape(shp), ig, fg)
    hm = _rmsnorm(hm, mlstm_norm).reshape(bsz, L, MLSTM_WIDTH).astype(xn.dtype)
    y_mlstm = hm * jax.nn.sigmoid(mo)

    return jnp.concatenate([y_attn, y_pool, y_mlstm], axis=-1) @ w_out


def _swiglu(h, wg, wu, wd):
    return (jax.nn.silu(h @ wg) * (h @ wu)) @ wd


def _moe(h, router_w, wg, wu, wd):
    bsz, L, D = h.shape
    xt = h.reshape(-1, D)
    N = xt.shape[0]
    logits = (xt @ router_w).astype(jnp.float32)
    top_val, top_idx = lax.top_k(logits, TOP_K)
    gates = jax.nn.softmax(top_val, axis=-1)
    A = N * TOP_K
    e_flat = top_idx.reshape(-1)
    tok_flat = jnp.repeat(jnp.arange(N, dtype=jnp.int32), TOP_K)
    g_flat = gates.reshape(-1)
    order = jnp.argsort(e_flat)
    e_sorted = e_flat[order]
    counts = jnp.bincount(e_flat, length=N_EXPERTS)
    padded = (counts + MOE_BLOCK - 1) // MOE_BLOCK * MOE_BLOCK
    start = jnp.cumsum(counts) - counts
    pend = jnp.cumsum(padded)
    pstart = pend - padded
    dest = pstart[e_sorted] + (jnp.arange(A) - start[e_sorted])
    n_blocks = -(-A // MOE_BLOCK) + N_EXPERTS
    R = n_blocks * MOE_BLOCK
    row_tok = jnp.zeros((R,), jnp.int32).at[dest].set(tok_flat[order])
    row_gate = jnp.zeros((R,), jnp.float32).at[dest].set(g_flat[order])
    block_exp = jnp.minimum(
        jnp.searchsorted(pend, jnp.arange(n_blocks) * MOE_BLOCK, side='right'), N_EXPERTS - 1)
    xs = xt[row_tok].reshape(n_blocks, MOE_BLOCK, D)
    ys = lax.map(lambda a: _swiglu(a[0], wg[a[1]], wu[a[1]], wd[a[1]]), (xs, block_exp))
    ys = ys.reshape(R, D) * row_gate[:, None].astype(xt.dtype)
    out = jnp.zeros_like(xt).at[row_tok].add(ys)
    return out.reshape(bsz, L, D)


def setup_inputs(seed: int = 0) -> dict:
    key = jax.random.key(seed)
    ks = jax.random.split(key, 32)
    n_dense = (DEPTH + 1) // 2
    n_moe = DEPTH // 2

    def nrm(k, shape, scale):
        return jax.random.normal(k, shape, jnp.float32) * scale

    f_bias = jnp.linspace(3.0, 6.0, MLSTM_HEADS, dtype=jnp.float32)[None, :] + nrm(ks[13], (DEPTH, MLSTM_HEADS), 0.1)
    i_bias = nrm(ks[12], (DEPTH, MLSTM_HEADS), 0.1)
    return {
        "x": nrm(ks[0], (BATCH, SEQ, D_MODEL), 1.0),
        "meta_tokens": nrm(ks[1], (N_META, D_MODEL), 1.0),
        "attn_norm": 1.0 + nrm(ks[2], (DEPTH, D_MODEL), 0.05),
        "w_in": nrm(ks[3], (DEPTH, D_MODEL, D_IN), D_MODEL ** -0.5),
        "q_norm": 1.0 + nrm(ks[4], (DEPTH, Q_LORA), 0.05),
        "w_q_up": nrm(ks[5], (DEPTH, Q_LORA, MLA_HEADS * (QK_NOPE + QK_ROPE)), Q_LORA ** -0.5),
        "kv_norm": 1.0 + nrm(ks[6], (DEPTH, KV_LORA), 0.05),
        "w_kv_up": nrm(ks[7], (DEPTH, KV_LORA, MLA_HEADS * (QK_NOPE + V_HEAD)), KV_LORA ** -0.5),
        "pool_w": nrm(ks[8], (DEPTH, POOL_GROUPS, POOL_GROUP_DIM, POOL_GROUP_DIM), POOL_GROUP_DIM ** -0.5),
        "pool_scale": 1.0 + nrm(ks[9], (DEPTH, POOL_WIDTH), 0.1),
        "conv_w": nrm(ks[10], (DEPTH, CONV_WIDTH, 2 * MLSTM_WIDTH), CONV_WIDTH ** -0.5),
        "conv_b": nrm(ks[11], (DEPTH, 2 * MLSTM_WIDTH), 0.01),
        "gate_bias": jnp.concatenate([i_bias, f_bias], axis=-1),
        "mlstm_norm": 1.0 + nrm(ks[14], (DEPTH, MLSTM_HEADS, MLSTM_HEAD_DIM), 0.05),
        "w_out": nrm(ks[15], (DEPTH, D_MIX, D_MODEL), D_MIX ** -0.5),
        "ffn_norm": 1.0 + nrm(ks[16], (DEPTH, D_MODEL), 0.05),
        "dense_w_gate": nrm(ks[17], (n_dense, D_MODEL, D_FF), D_MODEL ** -0.5),
        "dense_w_up": nrm(ks[18], (n_dense, D_MODEL, D_FF), D_MODEL ** -0.5),
        "dense_w_down": nrm(ks[19], (n_dense, D_FF, D_MODEL), D_FF ** -0.5),
        "router_w": nrm(ks[20], (n_moe, D_MODEL, N_EXPERTS), D_MODEL ** -0.5),
        "moe_w_gate": nrm(ks[21], (n_moe, N_EXPERTS, D_MODEL, D_FF_EXPERT), D_MODEL ** -0.5),
        "moe_w_up": nrm(ks[22], (n_moe, N_EXPERTS, D_MODEL, D_FF_EXPERT), D_MODEL ** -0.5),
        "moe_w_down": nrm(ks[23], (n_moe, N_EXPERTS, D_FF_EXPERT, D_MODEL), D_FF_EXPERT ** -0.5),
        "final_norm": 1.0 + nrm(ks[24], (D_MODEL,), 0.05),
    }


def reference(x, meta_tokens, attn_norm, w_in, q_norm, w_q_up, kv_norm, w_kv_up, pool_w,
              pool_scale, conv_w, conv_b, gate_bias, mlstm_norm, w_out, ffn_norm,
              dense_w_gate, dense_w_up, dense_w_down, router_w, moe_w_gate, moe_w_up,
              moe_w_down, final_norm):
    bsz = x.shape[0]
    meta = jnp.broadcast_to(meta_tokens[None].astype(x.dtype), (bsz, N_META, D_MODEL))
    h = jnp.concatenate([meta, x], axis=1)
    L = h.shape[1]
    pos = jnp.arange(L, dtype=jnp.int32)
    cid = (pos - N_META) // CHUNK + 1
    inv_freq = ROPE_THETA ** (-jnp.arange(0, QK_ROPE, 2, dtype=jnp.float32) / QK_ROPE)
    ang = pos.astype(jnp.float32)[:, None] * inv_freq[None, :]
    cos, sin = jnp.cos(ang), jnp.sin(ang)
    for l in range(DEPTH):
        h = h + _mixer(_rmsnorm(h, attn_norm[l]), w_in[l], q_norm[l], w_q_up[l], kv_norm[l],
                       w_kv_up[l], pool_w[l], pool_scale[l], conv_w[l], conv_b[l],
                       gate_bias[l], mlstm_norm[l], w_out[l], cos, sin, cid)
        hn = _rmsnorm(h, ffn_norm[l])
        if l % 2 == 0:
            j = l // 2
            h = h + _swiglu(hn, dense_w_gate[j], dense_w_up[j], dense_w_down[j])
        else:
            j = l // 2
            h = h + _moe(hn, router_w[j], moe_w_gate[j], moe_w_up[j], moe_w_down[j])
    return _rmsnorm(h, final_norm)[:, N_META:]
```

```python
import functools

import numpy as np
import jax
import jax.numpy as jnp
from jax import lax
from jax.experimental import pallas as pl
from jax.experimental.pallas import tpu as pltpu

F32 = jnp.float32
BF16 = jnp.bfloat16

D_MODEL = 1024
N_META = 16
CHUNK = 64
MLA_HEADS = 8
QK_NOPE = 64
QK_ROPE = 32
V_HEAD = 64
Q_LORA = 768
KV_LORA = 256
ROPE_THETA = 10000.0
POOL_GROUPS = 4
POOL_GROUP_DIM = 64
POOL_WIDTH = 256
MLSTM_HEADS = 4
MLSTM_HEAD_DIM = 64
MLSTM_WIDTH = 256
CONV_WIDTH = 4
D_FF = 2816
N_EXPERTS = 8
D_FF_EXPERT = 3584
EPS = 1e-6
NEG_INF = -1e30

LANE = 128
HEAD_PAD = 128
ATTN_TILE = 256
SEQ_CHUNK = 128
FLAT_TILE = 512
MOE_TILE = 512
FF_TILE_DENSE = 1408
FF_TILE_EXPERT = 512
VMEM_LIMIT = 56 * 1024 * 1024

SEG_CQ = (0, 768)
SEG_CKV = (768, 1024)
SEG_XP = (1024, 1280)
SEG_MQK = (1280, 1792)
SEG_MV = (1792, 2048)
SEG_MO = (2048, 2304)
SEG_SMALL = (2304, 2432)
W_BIG = 2432


def _rms(x, g):
    ms = jnp.mean(x * x, axis=-1, keepdims=True)
    return x * lax.rsqrt(ms + EPS) * g


def _dot(a, b):
    return jnp.dot(a, b, preferred_element_type=F32)


def _dot_nt(a, b):
    return lax.dot_general(a, b, (((1,), (1,)), ((), ())), preferred_element_type=F32)


def _dot_tn(a, b):
    return lax.dot_general(a, b, (((0,), (0,)), ((), ())), preferred_element_type=F32)


def _dot_f32(a, b):
    return jnp.dot(a, b, preferred_element_type=F32, precision=lax.Precision.HIGHEST)


def _log_sigmoid(x):
    return jnp.minimum(x, 0.0) - jnp.log(1.0 + jnp.exp(-jnp.abs(x)))


def _pick_tile(n, target):
    best = None
    for t in range(16, target + 1, 16):
        if n % t == 0:
            best = t
    assert best is not None, (n, target)
    return best


def _cparams(sem):
    return pltpu.CompilerParams(dimension_semantics=sem, vmem_limit_bytes=VMEM_LIMIT)


def _rope_lanes(x, c, s_lo, s_hi):
    return x * c + pltpu.roll(x, LANE - 16, 1) * s_lo + pltpu.roll(x, 16, 1) * s_hi


def _inproj_kernel(h_ref, an_ref, wbig_ref, qn_ref, wq_ref, kvn_ref, wk_ref, wv_ref, tab_ref,
                   q_ref, k_ref, v_ref, xp_ref, mqk_ref, mv_ref, mo_ref, zs_ref):
    xn = _rms(h_ref[0], an_ref[...]).astype(BF16)

    def seg(ab):
        return _dot(xn, wbig_ref[:, ab[0]:ab[1]])

    xp_ref[0] = seg(SEG_XP).astype(BF16)
    mqk_ref[0] = seg(SEG_MQK).astype(BF16)
    mv_ref[0] = seg(SEG_MV).astype(BF16)
    mo_ref[0] = seg(SEG_MO).astype(BF16)
    zs = seg(SEG_SMALL)
    zs_ref[0] = zs

    tab = tab_ref[...]
    cq, s1q, s2q = tab[:, 0:128], tab[:, 128:256], tab[:, 256:384]
    ck, s1k, s2k = tab[:, 384:512], tab[:, 512:640], tab[:, 640:768]

    cqn = _rms(seg(SEG_CQ), qn_ref[...]).astype(BF16)
    qf = _dot(cqn, wq_ref[...])
    for hd in range(MLA_HEADS):
        sl = slice(HEAD_PAD * hd, HEAD_PAD * (hd + 1))
        q_ref[0, :, sl] = _rope_lanes(qf[:, sl], cq, s1q, s2q).astype(BF16)

    ckvn = _rms(seg(SEG_CKV), kvn_ref[...]).astype(BF16)
    v_ref[0] = _dot(ckvn, wv_ref[...]).astype(BF16)
    kf = _dot(ckvn, wk_ref[...])
    kr = pltpu.roll(_rope_lanes(zs, ck, s1k, s2k), QK_NOPE, 1)
    for hd in range(MLA_HEADS):
        sl = slice(HEAD_PAD * hd, HEAD_PAD * (hd + 1))
        k_ref[0, :, sl] = (kf[:, sl] + kr).astype(BF16)


def _inproj(h, an, wbig, qn, wq, kvn, wk, wv, tab):
    B, L, D = h.shape
    T = _pick_tile(L, 768)
    nt = L // T
    row = lambda c: pl.BlockSpec((1, T, c), lambda b, i: (b, i, 0))
    full = lambda a: pl.BlockSpec(a.shape, lambda b, i: (0,) * a.ndim)
    widths = (MLA_HEADS * HEAD_PAD, MLA_HEADS * HEAD_PAD, MLA_HEADS * V_HEAD,
              POOL_WIDTH, 2 * MLSTM_WIDTH, MLSTM_WIDTH, MLSTM_WIDTH)
    out_shape = [jax.ShapeDtypeStruct((B, L, c), BF16) for c in widths]
    out_shape.append(jax.ShapeDtypeStruct((B, L, LANE), F32))
    out_specs = [row(c) for c in widths] + [row(LANE)]
    return pl.pallas_call(
        _inproj_kernel,
        grid=(B, nt),
        in_specs=[row(D), full(an), full(wbig), full(qn), full(wq), full(kvn), full(wk), full(wv),
                  pl.BlockSpec((T, tab.shape[1]), lambda b, i: (i, 0))],
        out_specs=out_specs,
        out_shape=out_shape,
        compiler_params=_cparams(("parallel", "parallel")),
        name="inproj",
    )(h, an, wbig, qn, wq, kvn, wk, wv, tab)


def _attn_kernel(q_ref, k_ref, v_ref, o_ref, s_ref, m_ref, l_ref, acc_ref, *, n_tiles):
    A = ATTN_TILE
    lane = lax.broadcasted_iota(jnp.int32, (1, LANE), 1)
    vmasks = (lane < V_HEAD, lane >= V_HEAD)

    def vsel(hh, vv):
        return jnp.where(vmasks[hh], vv, jnp.zeros_like(vv))

    o_meta = None
    for hh in range(2):
        lo = HEAD_PAD * hh
        s0 = _dot_nt(q_ref[0, 0:N_META, lo:lo + HEAD_PAD], k_ref[0, 0:N_META, lo:lo + HEAD_PAD])
        p0 = jnp.exp(s0 - jnp.max(s0, axis=-1, keepdims=True))
        o0 = _dot(p0.astype(BF16), vsel(hh, v_ref[0, 0:N_META, :])) / jnp.sum(p0, axis=-1, keepdims=True)
        o_meta = o0 if o_meta is None else o_meta + o0
    o_ref[0, 0:N_META, :] = o_meta.astype(o_ref.dtype)

    ri = lax.broadcasted_iota(jnp.int32, (A, A), 0) // CHUNK
    ci = lax.broadcasted_iota(jnp.int32, (A, A), 1) // CHUNK
    diag_mask = ci <= ri

    def rows(c):
        return pl.ds(pl.multiple_of(N_META + A * c, 16), A)

    def tile_body(j, carry):
        o_pair = None
        for hh in range(2):
            lo = HEAD_PAD * hh
            qj = q_ref[0, rows(j), lo:lo + HEAD_PAD]
            s_meta = _dot_nt(qj, k_ref[0, 0:N_META, lo:lo + HEAD_PAD])
            m_ref[hh] = jnp.full((A, LANE), NEG_INF, F32)

            def pass1(c, _):
                s = _dot_nt(qj, k_ref[0, rows(c), lo:lo + HEAD_PAD])
                s_ref[hh, c] = s
                m_ref[hh] = jnp.maximum(m_ref[hh], jnp.maximum(s[:, :LANE], s[:, LANE:]))
                return 0

            lax.fori_loop(0, j, pass1, 0)
            sd = _dot_nt(qj, k_ref[0, rows(j), lo:lo + HEAD_PAD])
            sd = jnp.where(diag_mask, sd, NEG_INF)
            s_ref[hh, j] = sd
            m128 = jnp.maximum(m_ref[hh], jnp.maximum(sd[:, :LANE], sd[:, LANE:]))
            m = jnp.maximum(jnp.max(m128, axis=-1, keepdims=True),
                            jnp.max(s_meta, axis=-1, keepdims=True))
            p_meta = jnp.exp(s_meta - m)
            l_meta = jnp.sum(p_meta, axis=-1, keepdims=True)
            acc_ref[hh] = _dot(p_meta.astype(BF16), vsel(hh, v_ref[0, 0:N_META, :]))
            l_ref[hh] = jnp.zeros((A, LANE), F32)

            def pass2(c, _):
                p = jnp.exp(s_ref[hh, c] - m)
                l_ref[hh] = l_ref[hh] + p[:, :LANE] + p[:, LANE:]
                acc_ref[hh] = acc_ref[hh] + _dot(p.astype(BF16), vsel(hh, v_ref[0, rows(c), :]))
                return 0

            lax.fori_loop(0, j + 1, pass2, 0)
            l = l_meta + jnp.sum(l_ref[hh], axis=-1, keepdims=True)
            o_h = acc_ref[hh] / l
            o_pair = o_h if o_pair is None else o_pair + o_h
        o_ref[0, rows(j), :] = o_pair.astype(o_ref.dtype)
        return carry

    lax.fori_loop(0, n_tiles, tile_body, 0)


def _attention(q, k, v):
    B, L, _ = q.shape
    n_tiles = (L - N_META) // ATTN_TILE
    assert N_META + n_tiles * ATTN_TILE == L
    n_pairs = MLA_HEADS // 2
    qk_spec = pl.BlockSpec((1, L, 2 * HEAD_PAD), lambda b, p: (b, 0, p))
    v_spec = pl.BlockSpec((1, L, 2 * V_HEAD), lambda b, p: (b, 0, p))
    return pl.pallas_call(
        functools.partial(_attn_kernel, n_tiles=n_tiles),
        grid=(B, n_pairs),
        in_specs=[qk_spec, qk_spec, v_spec],
        out_specs=v_spec,
        out_shape=jax.ShapeDtypeStruct((B, L, MLA_HEADS * V_HEAD), BF16),
        scratch_shapes=[pltpu.VMEM((2, n_tiles, ATTN_TILE, ATTN_TILE), F32),
                        pltpu.VMEM((2, ATTN_TILE, LANE), F32),
                        pltpu.VMEM((2, ATTN_TILE, LANE), F32),
                        pltpu.VMEM((2, ATTN_TILE, LANE), F32)],
        compiler_params=_cparams(("parallel", "parallel")),
        name="attention",
    )(q, k, v)


def _seq_kernel(xp_ref, mqk_ref, mv_ref, mo_ref, gcol_ref, grow_ref, pw_ref, ps_ref, cw_ref, cb_ref,
                gbc_ref, gbr_ref, mn_ref, yp_ref, ym_ref, ct_ref, m_ref, *, n_chunks):
    T = SEQ_CHUNK
    HIST = 16
    lane = lax.broadcasted_iota(jnp.int32, (1, LANE), 1)
    lane_lo = lane < MLSTM_HEAD_DIM
    lane256 = lax.broadcasted_iota(jnp.int32, (1, POOL_WIDTH), 1)
    grp = lane256 // POOL_GROUP_DIM
    win = jnp.left_shift(2, grp).astype(F32)
    ti = lax.broadcasted_iota(jnp.int32, (T, T), 0)
    si = lax.broadcasted_iota(jnp.int32, (T, T), 1)
    causal = si <= ti
    tril = causal.astype(F32)
    triu = (ti <= si).astype(F32)
    t_col = lax.broadcasted_iota(jnp.int32, (T, 1), 0)
    row8 = lax.broadcasted_iota(jnp.int32, (2 * MLSTM_HEADS, 1), 0)

    ct_ref[...] = jnp.zeros_like(ct_ref)
    m_ref[...] = jnp.zeros_like(m_ref)

    def chunk(c, xx_p, xx_c, mv_c, mo_c, pos0, store):
        s2 = xx_p + pltpu.roll(xx_p, 1, 0)
        s4 = s2 + pltpu.roll(s2, 2, 0)
        s8 = s4 + pltpu.roll(s4, 4, 0)
        s16 = s8 + pltpu.roll(s8, 8, 0)
        ssum = jnp.where(grp == 0, s2[HIST:], jnp.where(grp == 1, s4[HIST:],
                         jnp.where(grp == 2, s8[HIST:], s16[HIST:])))
        cnt = jnp.maximum(pos0 + t_col + 1, 1).astype(F32)
        d = ssum / jnp.minimum(cnt, win) - xx_p[HIST:]
        yp = _dot(d.astype(BF16), pw_ref[...]) * ps_ref[...]

        cw = cw_ref[...]
        y = (xx_c * cw[3:4] + pltpu.roll(xx_c, 1, 0) * cw[2:3] + pltpu.roll(xx_c, 2, 0) * cw[1:2]
             + pltpu.roll(xx_c, 3, 0) * cw[0:1])[HIST:] + cb_ref[...]
        qk = y * jax.nn.sigmoid(y)
        qc = qk[:, :MLSTM_WIDTH]
        kc = qk[:, MLSTM_WIDTH:] * (MLSTM_HEAD_DIM ** -0.5)

        gcol = gcol_ref[0, c] + gbc_ref[...]
        grow = grow_ref[0, c] + gbr_ref[...]
        is_f = jnp.logical_and(lane >= MLSTM_HEADS, lane < 2 * MLSTM_HEADS)
        b_col = _dot_f32(tril, jnp.where(is_f, _log_sigmoid(gcol), 0.0))
        b_row = _dot_f32(jnp.where(row8 >= MLSTM_HEADS, _log_sigmoid(grow), 0.0), triu)

        ym = []
        for pr in range(MLSTM_HEADS // 2):
            psl = slice(LANE * pr, LANE * (pr + 1))
            q_pair, k_pair = qc[:, psl], kc[:, psl]
            v_pair = mv_c[:, psl]
            h_heads = []
            for e in range(2):
                hd = 2 * pr + e
                own = lane_lo if e == 0 else jnp.logical_not(lane_lo)
                den_lane = MLSTM_HEAD_DIM if e == 0 else 0
                q_h = jnp.where(own, q_pair, 0.0).astype(BF16)
                k_h = jnp.where(own, k_pair, 0.0).astype(BF16)
                v_aug = jnp.where(own, v_pair, jnp.where(lane == den_lane, 1.0, 0.0))
                bc = b_col[:, MLSTM_HEADS + hd:MLSTM_HEADS + hd + 1]
                lic = gcol[:, hd:hd + 1]
                br = b_row[MLSTM_HEADS + hd:MLSTM_HEADS + hd + 1, :]
                lir = grow[hd:hd + 1, :]
                dm = jnp.where(causal, bc + (lir - br), NEG_INF)
                m_prev = m_ref[hd][:, 0:1]
                m_inter = bc + m_prev
                mt = jnp.maximum(m_inter, jnp.max(dm, axis=-1, keepdims=True))
                s = _dot_nt(q_h, k_h) * jnp.exp(dm - mt)
                decay = jnp.exp(m_inter - mt)
                num = (_dot(s.astype(BF16), v_aug.astype(BF16))
                       + decay * _dot(q_h, ct_ref[hd].astype(BF16)))
                den = num[:, den_lane:den_lane + 1]
                h_heads.append(num / jnp.maximum(jnp.abs(den), jnp.exp(-mt)))
                b_last = bc[T - 1:T, :]
                g_col = b_last - bc + lic
                m_new = jnp.maximum(b_last + m_prev, jnp.max(g_col, axis=0, keepdims=True))
                wg = jnp.exp(g_col - m_new)
                cd = jnp.exp(b_last + m_prev - m_new)
                ct_ref[hd] = cd * ct_ref[hd] + _dot_tn(k_h, (v_aug * wg).astype(BF16))
                m_ref[hd] = jnp.broadcast_to(m_new, (1, LANE))
            h_pair = jnp.where(lane_lo, h_heads[0], h_heads[1])
            sq = h_pair * h_pair
            ms = jnp.where(lane_lo,
                           jnp.sum(jnp.where(lane_lo, sq, 0.0), axis=-1, keepdims=True),
                           jnp.sum(jnp.where(lane_lo, 0.0, sq), axis=-1, keepdims=True)) / MLSTM_HEAD_DIM
            hn = h_pair * lax.rsqrt(ms + EPS) * mn_ref[:, psl]
            ym.append(hn * jax.nn.sigmoid(mo_c[:, psl]))
        store(yp, jnp.concatenate(ym, axis=-1))

    def first_rows(ref, width):
        real = ref[0, 0:N_META, :].astype(F32)
        return jnp.concatenate([jnp.zeros((T + HIST - N_META, width), F32), real], axis=0)

    def store_first(yp, ym):
        yp_ref[0, 0:N_META, :] = yp[T - N_META:].astype(yp_ref.dtype)
        ym_ref[0, 0:N_META, :] = ym[T - N_META:].astype(ym_ref.dtype)

    chunk(0, first_rows(xp_ref, POOL_WIDTH), first_rows(mqk_ref, 2 * MLSTM_WIDTH),
          first_rows(mv_ref, MLSTM_WIDTH)[HIST:], first_rows(mo_ref, MLSTM_WIDTH)[HIST:],
          N_META - T, store_first)

    def body(c, carry):
        r0 = pl.multiple_of(N_META + T * (c - 1), 16)
        ext = pl.ds(pl.multiple_of(r0 - HIST, 16), T + HIST)
        cur = pl.ds(r0, T)

        def store(yp, ym):
            yp_ref[0, cur, :] = yp.astype(yp_ref.dtype)
            ym_ref[0, cur, :] = ym.astype(ym_ref.dtype)

        chunk(c, xp_ref[0, ext, :].astype(F32), mqk_ref[0, ext, :].astype(F32),
              mv_ref[0, cur, :].astype(F32), mo_ref[0, cur, :].astype(F32), r0, store)
        return carry

    lax.fori_loop(1, n_chunks, body, 0)


def _seq_mixer(xp, mqk, mv, mo, gcol, grow, pw, ps, cw, cb, gbc, gbr, mn):
    B, L, _ = xp.shape
    n_chunks = gcol.shape[1]
    seq = lambda c: pl.BlockSpec((1, L, c), lambda b: (b, 0, 0))
    full = lambda a: pl.BlockSpec(a.shape, lambda b: (0,) * a.ndim)
    return pl.pallas_call(
        functools.partial(_seq_kernel, n_chunks=n_chunks),
        grid=(B,),
        in_specs=[seq(POOL_WIDTH), seq(2 * MLSTM_WIDTH), seq(MLSTM_WIDTH), seq(MLSTM_WIDTH),
                  pl.BlockSpec((1,) + gcol.shape[1:], lambda b: (b, 0, 0, 0)),
                  pl.BlockSpec((1,) + grow.shape[1:], lambda b: (b, 0, 0, 0)),
                  full(pw), full(ps), full(cw), full(cb), full(gbc), full(gbr), full(mn)],
        out_specs=[seq(POOL_WIDTH), seq(MLSTM_WIDTH)],
        out_shape=[jax.ShapeDtypeStruct((B, L, POOL_WIDTH), BF16),
                   jax.ShapeDtypeStruct((B, L, MLSTM_WIDTH), BF16)],
        scratch_shapes=[pltpu.VMEM((MLSTM_HEADS, LANE, LANE), F32),
                        pltpu.VMEM((MLSTM_HEADS, 1, LANE), F32)],
        compiler_params=_cparams(("parallel",)),
        name="seq_mixer",
    )(xp, mqk, mv, mo, gcol, grow, pw, ps, cw, cb, gbc, gbr, mn)


def _outproj_kernel(h_ref, ya_ref, yp_ref, ym_ref, wa_ref, wp_ref, wm_ref, o_ref):
    o_ref[...] = (h_ref[...] + _dot(ya_ref[...], wa_ref[...]) + _dot(yp_ref[...], wp_ref[...])
                  + _dot(ym_ref[...], wm_ref[...]))


def _outproj(h, ya, yp, ym, wa, wp, wm):
    N, D = h.shape
    T = _pick_tile(N, FLAT_TILE)
    row = lambda c: pl.BlockSpec((T, c), lambda i: (i, 0))
    full = lambda a: pl.BlockSpec(a.shape, lambda i: (0,) * a.ndim)
    return pl.pallas_call(
        _outproj_kernel,
        grid=(N // T,),
        in_specs=[row(D), row(ya.shape[1]), row(yp.shape[1]), row(ym.shape[1]), full(wa), full(wp), full(wm)],
        out_specs=row(D),
        out_shape=jax.ShapeDtypeStruct((N, D), F32),
        compiler_params=_cparams(("parallel",)),
        name="outproj",
    )(h, ya, yp, ym, wa, wp, wm)


def _ffn_kernel(h_ref, g_ref, wg_ref, wu_ref, wd_ref, o_ref, xn_ref, acc_ref):
    j = pl.program_id(1)

    @pl.when(j == 0)
    def _():
        xn_ref[...] = _rms(h_ref[...], g_ref[...]).astype(BF16)
        acc_ref[...] = h_ref[...]

    xn = xn_ref[...]
    a = _dot(xn, wg_ref[...])
    u = _dot(xn, wu_ref[...])
    acc_ref[...] += _dot((a * jax.nn.sigmoid(a) * u).astype(BF16), wd_ref[...])

    @pl.when(j == pl.num_programs(1) - 1)
    def _():
        o_ref[...] = acc_ref[...]


def _ffn(h, g, wg, wu, wd):
    N, D = h.shape
    FF = wg.shape[1]
    T = _pick_tile(N, FLAT_TILE)
    TF = FF_TILE_DENSE
    assert FF % TF == 0
    return pl.pallas_call(
        _ffn_kernel,
        grid=(N // T, FF // TF),
        in_specs=[pl.BlockSpec((T, D), lambda i, j: (i, 0)),
                  pl.BlockSpec(g.shape, lambda i, j: (0, 0)),
                  pl.BlockSpec((D, TF), lambda i, j: (0, j)),
                  pl.BlockSpec((D, TF), lambda i, j: (0, j)),
                  pl.BlockSpec((TF, D), lambda i, j: (j, 0))],
        out_specs=pl.BlockSpec((T, D), lambda i, j: (i, 0)),
        out_shape=jax.ShapeDtypeStruct((N, D), F32),
        scratch_shapes=[pltpu.VMEM((T, D), BF16), pltpu.VMEM((T, D), F32)],
        compiler_params=_cparams(("parallel", "arbitrary")),
        name="dense_ffn",
    )(h, g, wg, wu, wd)


def _router_kernel(h_ref, g_ref, wr_ref, route_ref, cnt_ref, run_ref):
    i = pl.program_id(0)
    T = h_ref.shape[0]

    @pl.when(i == 0)
    def _():
        run_ref[...] = jnp.zeros_like(run_ref)

    xn = _rms(h_ref[...], g_ref[...])
    lane = lax.broadcasted_iota(jnp.int32, (T, LANE), 1).astype(F32)
    logits = jnp.where(lane < N_EXPERTS, _dot_f32(xn, wr_ref[...]), NEG_INF)
    v1 = jnp.max(logits, axis=-1, keepdims=True)
    i1 = jnp.min(jnp.where(logits == v1, lane, float(LANE)), axis=-1, keepdims=True)
    rest = jnp.where(lane == i1, NEG_INF, logits)
    v2 = jnp.max(rest, axis=-1, keepdims=True)
    i2 = jnp.min(jnp.where(rest == v2, lane, float(LANE)), axis=-1, keepdims=True)
    e2 = jnp.exp(v2 - v1)
    g1 = 1.0 / (1.0 + e2)
    g2 = e2 / (1.0 + e2)
    member = jnp.logical_or(lane == i1, lane == i2).astype(F32)
    ri = lax.broadcasted_iota(jnp.int32, (T, T), 0)
    ci = lax.broadcasted_iota(jnp.int32, (T, T), 1)
    strict = jnp.where(ci < ri, 1.0, 0.0).astype(BF16)
    rank = run_ref[0:1, :] + _dot(strict, member.astype(BF16))
    r1 = jnp.sum(jnp.where(lane == i1, rank, 0.0), axis=-1, keepdims=True)
    r2 = jnp.sum(jnp.where(lane == i2, rank, 0.0), axis=-1, keepdims=True)
    run_ref[...] = run_ref[...] + jnp.sum(member, axis=0, keepdims=True)
    cnt_ref[...] = run_ref[...]
    route = jnp.where(lane == 0, i1, jnp.where(lane == 1, i2,
            jnp.where(lane == 2, r1, jnp.where(lane == 3, r2,
            jnp.where(lane == 4, g1, jnp.where(lane == 5, g2, 0.0))))))
    route_ref[...] = route


def _router(h, g, wr):
    N, D = h.shape
    T = _pick_tile(N, FLAT_TILE)
    return pl.pallas_call(
        _router_kernel,
        grid=(N // T,),
        in_specs=[pl.BlockSpec((T, D), lambda i: (i, 0)),
                  pl.BlockSpec(g.shape, lambda i: (0, 0)),
                  pl.BlockSpec(wr.shape, lambda i: (0, 0))],
        out_specs=[pl.BlockSpec((T, LANE), lambda i: (i, 0)),
                   pl.BlockSpec((8, LANE), lambda i: (0, 0))],
        out_shape=[jax.ShapeDtypeStruct((N, LANE), F32), jax.ShapeDtypeStruct((8, LANE), F32)],
        scratch_shapes=[pltpu.VMEM((8, LANE), F32)],
        compiler_params=_cparams(("arbitrary",)),
        name="router",
    )(h, g, wr)


def _dispatch_kernel(dest_ref, h_ref, xs_in_ref, xs_ref, sem):
    del xs_in_ref
    i = pl.program_id(0)
    n = dest_ref.shape[2] // 2
    base = i * n

    def copy(t, slot):
        return pltpu.make_async_copy(h_ref.at[pl.ds(base + t, 1)],
                                     xs_ref.at[pl.ds(dest_ref[0, 0, 2 * t + slot], 1)], sem)

    def issue(t, c):
        copy(t, 0).start()
        copy(t, 1).start()
        return c

    lax.fori_loop(0, n, issue, 0)

    def drain(t, c):
        copy(t, 0).wait()
        copy(t, 1).wait()
        return c

    lax.fori_loop(0, n, drain, 0)


def _dispatch(dest3, h, xs_init):
    n_tiles = dest3.shape[0]
    return pl.pallas_call(
        _dispatch_kernel,
        grid=(n_tiles,),
        in_specs=[pl.BlockSpec((1, 1, dest3.shape[2]), lambda i: (i, 0, 0), memory_space=pltpu.SMEM),
                  pl.BlockSpec(memory_space=pl.ANY),
                  pl.BlockSpec(memory_space=pl.ANY)],
        out_specs=pl.BlockSpec(memory_space=pl.ANY),
        out_shape=jax.ShapeDtypeStruct(xs_init.shape, xs_init.dtype),
        scratch_shapes=[pltpu.SemaphoreType.DMA(())],
        input_output_aliases={2: 0},
        compiler_params=_cparams(("arbitrary",)),
        name="dispatch",
    )(dest3, h, xs_init)


def _expert_kernel(be_ref, x_ref, g_ref, wg_ref, wu_ref, wd_ref, o_ref, xn_ref, acc_ref):
    del be_ref
    j = pl.program_id(1)

    @pl.when(j == 0)
    def _():
        xn_ref[...] = _rms(x_ref[...], g_ref[...]).astype(BF16)
        acc_ref[...] = jnp.zeros_like(acc_ref)

    xn = xn_ref[...]
    a = _dot(xn, wg_ref[0])
    u = _dot(xn, wu_ref[0])
    acc_ref[...] += _dot((a * jax.nn.sigmoid(a) * u).astype(BF16), wd_ref[0])

    @pl.when(j == pl.num_programs(1) - 1)
    def _():
        o_ref[...] = acc_ref[...]


def _experts(block_exp, xs, g, wg, wu, wd):
    R, D = xs.shape
    TM = MOE_TILE
    FF = wg.shape[2]
    TF = FF_TILE_EXPERT
    assert FF % TF == 0 and R % TM == 0
    grid_spec = pltpu.PrefetchScalarGridSpec(
        num_scalar_prefetch=1,
        grid=(R // TM, FF // TF),
        in_specs=[pl.BlockSpec((TM, D), lambda r, j, be: (r, 0)),
                  pl.BlockSpec(g.shape, lambda r, j, be: (0, 0)),
                  pl.BlockSpec((1, D, TF), lambda r, j, be: (be[r], 0, j)),
                  pl.BlockSpec((1, D, TF), lambda r, j, be: (be[r], 0, j)),
                  pl.BlockSpec((1, TF, D), lambda r, j, be: (be[r], j, 0))],
        out_specs=pl.BlockSpec((TM, D), lambda r, j, be: (r, 0)),
        scratch_shapes=[pltpu.VMEM((TM, D), BF16), pltpu.VMEM((TM, D), F32)],
    )
    return pl.pallas_call(
        _expert_kernel,
        grid_spec=grid_spec,
        out_shape=jax.ShapeDtypeStruct((R, D), F32),
        compiler_params=_cparams(("parallel", "arbitrary")),
        name="experts",
    )(block_exp, xs, g, wg, wu, wd)


def _combine_kernel(dest_ref, h_ref, route_ref, ys_ref, o_ref, buf_ref, sem):
    T = h_ref.shape[0]

    def copy(t, slot):
        return pltpu.make_async_copy(ys_ref.at[pl.ds(dest_ref[0, 0, 2 * t + slot], 1)],
                                     buf_ref.at[slot, pl.ds(t, 1)], sem)

    def issue(t, c):
        copy(t, 0).start()
        copy(t, 1).start()
        return c

    lax.fori_loop(0, T, issue, 0)

    def drain(t, c):
        copy(t, 0).wait()
        copy(t, 1).wait()
        return c

    lax.fori_loop(0, T, drain, 0)
    route = route_ref[...]
    o_ref[...] = h_ref[...] + route[:, 4:5] * buf_ref[0] + route[:, 5:6] * buf_ref[1]


def _combine(dest3, h, route, ys):
    N, D = h.shape
    T = dest3.shape[2] // 2
    return pl.pallas_call(
        _combine_kernel,
        grid=(N // T,),
        in_specs=[pl.BlockSpec((1, 1, 2 * T), lambda i: (i, 0, 0), memory_space=pltpu.SMEM),
                  pl.BlockSpec((T, D), lambda i: (i, 0)),
                  pl.BlockSpec((T, LANE), lambda i: (i, 0)),
                  pl.BlockSpec(memory_space=pl.ANY)],
        out_specs=pl.BlockSpec((T, D), lambda i: (i, 0)),
        out_shape=jax.ShapeDtypeStruct((N, D), F32),
        scratch_shapes=[pltpu.VMEM((2, T, D), F32), pltpu.SemaphoreType.DMA(())],
        compiler_params=_cparams(("arbitrary",)),
        name="combine",
    )(dest3, h, route, ys)


def _moe(h, g, wr, wg, wu, wd):
    N, D = h.shape
    T = _pick_tile(N, FLAT_TILE)
    route, counts = _router(h, g, wr)
    cnt = counts[0, :N_EXPERTS].astype(jnp.int32)
    padded = (cnt + MOE_TILE - 1) // MOE_TILE * MOE_TILE
    pend = jnp.cumsum(padded)
    pstart = pend - padded
    idx = route[:, 0:2].astype(jnp.int32)
    dest = pstart[idx] + route[:, 2:4].astype(jnp.int32)
    dest3 = dest.reshape(N // T, 1, 2 * T)
    n_blocks = -(-2 * N // MOE_TILE) + N_EXPERTS
    block_exp = jnp.minimum(jnp.searchsorted(pend, jnp.arange(n_blocks) * MOE_TILE, side='right'),
                            N_EXPERTS - 1).astype(jnp.int32)
    xs = _dispatch(dest3, h, jnp.zeros((n_blocks * MOE_TILE, D), F32))
    ys = _experts(block_exp, xs, g, wg, wu, wd)
    return _combine(dest3, h, route, ys)


def _final_kernel(h_ref, g_ref, o_ref):
    o_ref[0] = _rms(h_ref[0, N_META:, :], g_ref[...])


def _final(h, g):
    B, L, D = h.shape
    return pl.pallas_call(
        _final_kernel,
        grid=(B,),
        in_specs=[pl.BlockSpec((1, L, D), lambda b: (b, 0, 0)), pl.BlockSpec(g.shape, lambda b: (0, 0))],
        out_specs=pl.BlockSpec((1, L - N_META, D), lambda b: (b, 0, 0)),
        out_shape=jax.ShapeDtypeStruct((B, L - N_META, D), F32),
        compiler_params=_cparams(("parallel",)),
        name="final_norm",
    )(h, g)


def _rope_tables(L):
    pos = jnp.arange(L, dtype=jnp.int32)
    inv_freq = ROPE_THETA ** (-jnp.arange(0, QK_ROPE, 2, dtype=F32) / QK_ROPE)
    ang = pos.astype(F32)[:, None] * inv_freq[None, :]
    cos, sin = jnp.cos(ang), jnp.sin(ang)
    z = lambda n: jnp.zeros((L, n), F32)
    scale = (QK_NOPE + QK_ROPE) ** -0.5
    cq = jnp.concatenate([jnp.ones((L, QK_NOPE), F32), cos, cos, z(32)], axis=1) * scale
    s1q = jnp.concatenate([z(QK_NOPE), -sin, z(48)], axis=1) * scale
    s2q = jnp.concatenate([z(QK_NOPE + 16), sin, z(32)], axis=1) * scale
    ck = jnp.concatenate([cos, cos, z(96)], axis=1)
    s1k = jnp.concatenate([-sin, z(112)], axis=1)
    s2k = jnp.concatenate([z(16), sin, z(96)], axis=1)
    return jnp.concatenate([cq, s1q, s2q, ck, s1k, s2k], axis=1)


def _layer_weights(l, w_in, w_q_up, w_kv_up, pool_w, w_out):
    pts = np.cumsum([Q_LORA, KV_LORA, QK_ROPE, POOL_WIDTH, 2 * MLSTM_WIDTH, MLSTM_WIDTH, MLSTM_WIDTH]).tolist()
    wi = w_in[l]
    w_cq, w_ckv, w_kr, w_xp, w_mqk, w_mv, w_mo, w_mg = jnp.split(wi, pts, axis=1)
    small = jnp.concatenate([w_kr, w_mg, jnp.zeros((D_MODEL, LANE - QK_ROPE - 2 * MLSTM_HEADS), F32)], axis=1)
    wbig = jnp.concatenate([w_cq, w_ckv, w_xp, w_mqk, w_mv, w_mo, small], axis=1).astype(BF16)
    wq = w_q_up[l].reshape(Q_LORA, MLA_HEADS, QK_NOPE + QK_ROPE)
    wq = jnp.pad(wq, ((0, 0), (0, 0), (0, HEAD_PAD - QK_NOPE - QK_ROPE))).reshape(Q_LORA, -1).astype(BF16)
    wkv = w_kv_up[l].reshape(KV_LORA, MLA_HEADS, QK_NOPE + V_HEAD)
    wk = jnp.pad(wkv[:, :, :QK_NOPE], ((0, 0), (0, 0), (0, HEAD_PAD - QK_NOPE))).reshape(KV_LORA, -1).astype(BF16)
    wv = wkv[:, :, QK_NOPE:].reshape(KV_LORA, -1).astype(BF16)
    pw = jax.scipy.linalg.block_diag(*[pool_w[l, g] for g in range(POOL_GROUPS)]).astype(BF16)
    wo = w_out[l].astype(BF16)
    n_attn = MLA_HEADS * V_HEAD
    return dict(wbig=wbig, wq=wq, wk=wk, wv=wv, pw=pw,
                wa=wo[:n_attn], wp=wo[n_attn:n_attn + POOL_WIDTH], wm=wo[n_attn + POOL_WIDTH:])


def kernel(x, meta_tokens, attn_norm, w_in, q_norm, w_q_up, kv_norm, w_kv_up, pool_w, pool_scale, conv_w, conv_b, gate_bias, mlstm_norm, w_out, ffn_norm, dense_w_gate, dense_w_up, dense_w_down, router_w, moe_w_gate, moe_w_up, moe_w_down, final_norm):
    B, S, D = x.shape
    L = N_META + S
    N = B * L
    depth = w_in.shape[0]
    n_chunks = 1 + S // SEQ_CHUNK
    assert S % ATTN_TILE == 0 and S % SEQ_CHUNK == 0

    meta = jnp.broadcast_to(meta_tokens[None].astype(x.dtype), (B, N_META, D))
    h = jnp.concatenate([meta, x], axis=1)
    tab = _rope_tables(L)
    row = lambda a: a.reshape(1, -1).astype(F32)

    for l in range(depth):
        lw = _layer_weights(l, w_in, w_q_up, w_kv_up, pool_w, w_out)
        q, k, v, xp, mqk, mv, mo, zs = _inproj(h, row(attn_norm[l]), lw['wbig'], row(q_norm[l]), lw['wq'],
                                               row(kv_norm[l]), lw['wk'], lw['wv'], tab)
        ya = _attention(q, k, v)
        gates = zs[:, :, QK_ROPE:QK_ROPE + 2 * MLSTM_HEADS]
        padv = jnp.concatenate([jnp.full((MLSTM_HEADS,), NEG_INF, F32), jnp.full((MLSTM_HEADS,), 1e4, F32)])
        gpad = jnp.broadcast_to(padv, (B, SEQ_CHUNK - N_META, 2 * MLSTM_HEADS))
        gates = jnp.concatenate([gpad, gates], axis=1).reshape(B, n_chunks, SEQ_CHUNK, 2 * MLSTM_HEADS)
        gcol = jnp.pad(gates, ((0, 0), (0, 0), (0, 0), (0, LANE - 2 * MLSTM_HEADS)))
        grow = gates.transpose(0, 1, 3, 2)
        gbc = jnp.pad(gate_bias[l], (0, LANE - 2 * MLSTM_HEADS)).reshape(1, LANE).astype(F32)
        gbr = gate_bias[l].reshape(2 * MLSTM_HEADS, 1).astype(F32)
        yp, ym = _seq_mixer(xp, mqk, mv, mo, gcol, grow, lw['pw'], row(pool_scale[l]),
                            conv_w[l].astype(F32), row(conv_b[l]), gbc, gbr, row(mlstm_norm[l]))
        hf = _outproj(h.reshape(N, D), ya.reshape(N, -1), yp.reshape(N, -1), ym.reshape(N, -1),
                      lw['wa'], lw['wp'], lw['wm'])
        j = l // 2
        if l % 2 == 0:
            hf = _ffn(hf, row(ffn_norm[l]), dense_w_gate[j].astype(BF16), dense_w_up[j].astype(BF16),
                      dense_w_down[j].astype(BF16))
        else:
            wr = jnp.pad(router_w[j], ((0, 0), (0, LANE - N_EXPERTS))).astype(F32)
            hf = _moe(hf, row(ffn_norm[l]), wr, moe_w_gate[j].astype(BF16), moe_w_up[j].astype(BF16),
                      moe_w_down[j].astype(BF16))
        h = hf.reshape(B, L, D)
    return _final(h, row(final_norm))
```

```python
import functools

import numpy as np
import jax
import jax.numpy as jnp
from jax import lax
from jax.experimental import pallas as pl
from jax.experimental.pallas import tpu as pltpu

F32 = jnp.float32
BF16 = jnp.bfloat16

D_MODEL = 1024
N_META = 16
CHUNK = 64
MLA_HEADS = 8
QK_NOPE = 64
QK_ROPE = 32
V_HEAD = 64
Q_LORA = 768
KV_LORA = 256
ROPE_THETA = 10000.0
POOL_GROUPS = 4
POOL_GROUP_DIM = 64
POOL_WIDTH = 256
MLSTM_HEADS = 4
MLSTM_HEAD_DIM = 64
MLSTM_WIDTH = 256
CONV_WIDTH = 4
D_FF = 2816
N_EXPERTS = 8
D_FF_EXPERT = 3584
EPS = 1e-6
NEG_INF = -1e30

LANE = 128
HEAD_PAD = 128
ATTN_TILE = 256
SEQ_CHUNK = 128
SEQ_BATCH = 2
FLAT_TILE = 512
FFN_ROW_TILE = 768
MOE_TILE = 1024
FF_TILE_DENSE = 1408
FF_TILE_EXPERT = 896
VMEM_LIMIT = 56 * 1024 * 1024

SEG_CQ = (0, 768)
SEG_CKV = (768, 1024)
SEG_XP = (1024, 1280)
SEG_MQK = (1280, 1792)
SEG_MV = (1792, 2048)
SEG_MO = (2048, 2304)
SEG_SMALL = (2304, 2432)
W_BIG = 2432


def _rms(x, g):
    ms = jnp.mean(x * x, axis=-1, keepdims=True)
    return x * lax.rsqrt(ms + EPS) * g


def _dot(a, b):
    return jnp.dot(a, b, preferred_element_type=F32)


def _dot_nt(a, b):
    return lax.dot_general(a, b, (((1,), (1,)), ((), ())), preferred_element_type=F32)


def _dot_tn(a, b):
    return lax.dot_general(a, b, (((0,), (0,)), ((), ())), preferred_element_type=F32)


def _dot_f32(a, b):
    return jnp.dot(a, b, preferred_element_type=F32, precision=lax.Precision.HIGHEST)


def _log_sigmoid(x):
    return jnp.minimum(x, 0.0) - jnp.log(1.0 + jnp.exp(-jnp.abs(x)))


def _pick_tile(n, target):
    best = None
    for t in range(16, target + 1, 16):
        if n % t == 0:
            best = t
    assert best is not None, (n, target)
    return best


def _cparams(sem):
    return pltpu.CompilerParams(dimension_semantics=sem, vmem_limit_bytes=VMEM_LIMIT)


def _rope_lanes(x, c, s_lo, s_hi):
    return x * c + pltpu.roll(x, LANE - 16, 1) * s_lo + pltpu.roll(x, 16, 1) * s_hi


def _inproj_kernel(h_ref, an_ref, wbig_ref, qn_ref, wq_ref, kvn_ref, wk_ref, wv_ref, tab_ref,
                   q_ref, k_ref, v_ref, xp_ref, mqk_ref, mv_ref, mo_ref, zs_ref):
    xn = _rms(h_ref[0], an_ref[...]).astype(BF16)

    def seg(ab):
        return _dot(xn, wbig_ref[:, ab[0]:ab[1]])

    xp_ref[0] = seg(SEG_XP).astype(BF16)
    mqk_ref[0] = seg(SEG_MQK).astype(BF16)
    mv_ref[0] = seg(SEG_MV).astype(BF16)
    mo_ref[0] = seg(SEG_MO).astype(BF16)
    zs = seg(SEG_SMALL)
    zs_ref[0] = zs

    tab = tab_ref[...]
    cq, s1q, s2q = tab[:, 0:128], tab[:, 128:256], tab[:, 256:384]
    ck, s1k, s2k = tab[:, 384:512], tab[:, 512:640], tab[:, 640:768]

    cqn = _rms(seg(SEG_CQ), qn_ref[...]).astype(BF16)
    qf = _dot(cqn, wq_ref[...])
    for hd in range(MLA_HEADS):
        sl = slice(HEAD_PAD * hd, HEAD_PAD * (hd + 1))
        q_ref[0, hd] = _rope_lanes(qf[:, sl], cq, s1q, s2q).astype(BF16)

    ckvn = _rms(seg(SEG_CKV), kvn_ref[...]).astype(BF16)
    v_ref[0] = _dot(ckvn, wv_ref[...]).astype(BF16)
    kf = _dot(ckvn, wk_ref[...])
    kr = pltpu.roll(_rope_lanes(zs, ck, s1k, s2k), QK_NOPE, 1)
    for hd in range(MLA_HEADS):
        sl = slice(HEAD_PAD * hd, HEAD_PAD * (hd + 1))
        k_ref[0, hd] = (kf[:, sl] + kr).astype(BF16)


def _inproj(h, an, wbig, qn, wq, kvn, wk, wv, tab):
    B, L, D = h.shape
    T = _pick_tile(L, 768)
    nt = L // T
    row = lambda c: pl.BlockSpec((1, T, c), lambda b, i: (b, i, 0))
    full = lambda a: pl.BlockSpec(a.shape, lambda b, i: (0,) * a.ndim)
    widths = (MLA_HEADS * V_HEAD, POOL_WIDTH, 2 * MLSTM_WIDTH, MLSTM_WIDTH, MLSTM_WIDTH)
    head_major = jax.ShapeDtypeStruct((B, MLA_HEADS, L, HEAD_PAD), BF16)
    head_spec = pl.BlockSpec((1, MLA_HEADS, T, HEAD_PAD), lambda b, i: (b, 0, i, 0))
    out_shape = [head_major, head_major] + [jax.ShapeDtypeStruct((B, L, c), BF16) for c in widths]
    out_shape.append(jax.ShapeDtypeStruct((B, L, LANE), F32))
    out_specs = [head_spec, head_spec] + [row(c) for c in widths] + [row(LANE)]
    return pl.pallas_call(
        _inproj_kernel,
        grid=(B, nt),
        in_specs=[row(D), full(an), full(wbig), full(qn), full(wq), full(kvn), full(wk), full(wv),
                  pl.BlockSpec((T, tab.shape[1]), lambda b, i: (i, 0))],
        out_specs=out_specs,
        out_shape=out_shape,
        compiler_params=_cparams(("parallel", "parallel")),
        name="inproj",
    )(h, an, wbig, qn, wq, kvn, wk, wv, tab)


def _attn_kernel(q_ref, k_ref, v_ref, o_ref, vm_ref, s_ref, p_ref, oacc_ref, *, n_tiles):
    A = ATTN_TILE
    lane = lax.broadcasted_iota(jnp.int32, (1, LANE), 1)
    ri = lax.broadcasted_iota(jnp.int32, (A, A), 0) // CHUNK
    ci = lax.broadcasted_iota(jnp.int32, (A, A), 1) // CHUNK
    diag_mask = ci <= ri
    oacc_ref[...] = jnp.zeros_like(oacc_ref)

    def head_body(hh, carry):
        vm_ref[...] = jnp.where(lane // V_HEAD == hh, v_ref[0], jnp.zeros_like(v_ref[0]))

        s0 = _dot_nt(q_ref[0, hh, 0:N_META, :], k_ref[0, hh, 0:N_META, :])
        p0 = jnp.exp(s0 - jnp.max(s0, axis=-1, keepdims=True))
        o0 = _dot(p0.astype(BF16), vm_ref[0:N_META, :]) / jnp.sum(p0, axis=-1, keepdims=True)
        oacc_ref[0:N_META, :] += o0

        for j in range(n_tiles):
            r0 = N_META + A * j
            qj = q_ref[0, hh, r0:r0 + A, :]
            s_meta = _dot_nt(qj, k_ref[0, hh, 0:N_META, :])
            m128 = None
            for c in range(j + 1):
                s = _dot_nt(qj, k_ref[0, hh, N_META + A * c:N_META + A * (c + 1), :])
                if c == j:
                    s = jnp.where(diag_mask, s, NEG_INF)
                s_ref[:, A * c:A * (c + 1)] = s
                mc = jnp.maximum(s[:, :LANE], s[:, LANE:])
                m128 = mc if m128 is None else jnp.maximum(m128, mc)
            m = jnp.maximum(jnp.max(m128, axis=-1, keepdims=True),
                            jnp.max(s_meta, axis=-1, keepdims=True))
            p_meta = jnp.exp(s_meta - m)
            l128 = jnp.zeros((A, LANE), F32)
            for c in range(j + 1):
                p = jnp.exp(s_ref[:, A * c:A * (c + 1)] - m)
                l128 = l128 + p[:, :LANE] + p[:, LANE:]
                p_ref[:, A * c:A * (c + 1)] = p.astype(BF16)
            l = jnp.sum(p_meta, axis=-1, keepdims=True) + jnp.sum(l128, axis=-1, keepdims=True)
            n = A * (j + 1)
            acc = (_dot(p_ref[:, 0:n], vm_ref[N_META:N_META + n, :])
                   + _dot(p_meta.astype(BF16), vm_ref[0:N_META, :]))
            oacc_ref[r0:r0 + A, :] += acc / l
        return carry

    lax.fori_loop(0, 2, head_body, 0)
    o_ref[0] = oacc_ref[...].astype(o_ref.dtype)


def _attention(q, k, v):
    B, _, L, _ = q.shape
    n_tiles = (L - N_META) // ATTN_TILE
    assert N_META + n_tiles * ATTN_TILE == L
    n_pairs = MLA_HEADS // 2
    qk_spec = pl.BlockSpec((1, 2, L, HEAD_PAD), lambda b, p: (b, p, 0, 0))
    v_spec = pl.BlockSpec((1, L, 2 * V_HEAD), lambda b, p: (b, 0, p))
    return pl.pallas_call(
        functools.partial(_attn_kernel, n_tiles=n_tiles),
        grid=(B, n_pairs),
        in_specs=[qk_spec, qk_spec, v_spec],
        out_specs=v_spec,
        out_shape=jax.ShapeDtypeStruct((B, L, MLA_HEADS * V_HEAD), BF16),
        scratch_shapes=[pltpu.VMEM((L, 2 * V_HEAD), BF16),
                        pltpu.VMEM((ATTN_TILE, n_tiles * ATTN_TILE), F32),
                        pltpu.VMEM((ATTN_TILE, n_tiles * ATTN_TILE), BF16),
                        pltpu.VMEM((L, 2 * V_HEAD), F32)],
        compiler_params=_cparams(("parallel", "parallel")),
        name="attention",
    )(q, k, v)


def _seq_kernel(xp_ref, mqk_ref, mv_ref, mo_ref, gcol_ref, grow_ref, pw_ref, ps_ref, cw_ref, cb_ref,
                gbc_ref, gbr_ref, mn_ref, yp_ref, ym_ref, ct_ref, m_ref, *, n_chunks, n_batch):
    T = SEQ_CHUNK
    HIST = 16
    lane = lax.broadcasted_iota(jnp.int32, (1, LANE), 1)
    lane_lo = lane < MLSTM_HEAD_DIM
    lane256 = lax.broadcasted_iota(jnp.int32, (1, POOL_WIDTH), 1)
    grp = lane256 // POOL_GROUP_DIM
    win = jnp.left_shift(2, grp).astype(F32)
    ti = lax.broadcasted_iota(jnp.int32, (T, T), 0)
    si = lax.broadcasted_iota(jnp.int32, (T, T), 1)
    causal = si <= ti
    tril = causal.astype(F32)
    triu = (ti <= si).astype(F32)
    t_col = lax.broadcasted_iota(jnp.int32, (T, 1), 0)
    row8 = lax.broadcasted_iota(jnp.int32, (2 * MLSTM_HEADS, 1), 0)

    ct_ref[...] = jnp.zeros_like(ct_ref)
    m_ref[...] = jnp.zeros_like(m_ref)

    def chunk(bb, c, xx_p, xx_c, mv_c, mo_c, pos0, store):
        s2 = xx_p + pltpu.roll(xx_p, 1, 0)
        s4 = s2 + pltpu.roll(s2, 2, 0)
        s8 = s4 + pltpu.roll(s4, 4, 0)
        s16 = s8 + pltpu.roll(s8, 8, 0)
        ssum = jnp.where(grp == 0, s2[HIST:], jnp.where(grp == 1, s4[HIST:],
                         jnp.where(grp == 2, s8[HIST:], s16[HIST:])))
        cnt = jnp.maximum(pos0 + t_col + 1, 1).astype(F32)
        d = ssum / jnp.minimum(cnt, win) - xx_p[HIST:]
        yp = _dot(d.astype(BF16), pw_ref[...]) * ps_ref[...]

        cw = cw_ref[...]
        y = (xx_c * cw[3:4] + pltpu.roll(xx_c, 1, 0) * cw[2:3] + pltpu.roll(xx_c, 2, 0) * cw[1:2]
             + pltpu.roll(xx_c, 3, 0) * cw[0:1])[HIST:] + cb_ref[...]
        qk = y * jax.nn.sigmoid(y)
        qc = qk[:, :MLSTM_WIDTH]
        kc = qk[:, MLSTM_WIDTH:] * (MLSTM_HEAD_DIM ** -0.5)

        gcol = gcol_ref[bb, c] + gbc_ref[...]
        grow = grow_ref[bb, c] + gbr_ref[...]
        is_f = jnp.logical_and(lane >= MLSTM_HEADS, lane < 2 * MLSTM_HEADS)
        b_col = _dot_f32(tril, jnp.where(is_f, _log_sigmoid(gcol), 0.0))
        b_row = _dot_f32(jnp.where(row8 >= MLSTM_HEADS, _log_sigmoid(grow), 0.0), triu)

        ym = []
        for pr in range(MLSTM_HEADS // 2):
            psl = slice(LANE * pr, LANE * (pr + 1))
            q_pair, k_pair = qc[:, psl], kc[:, psl]
            v_pair = mv_c[:, psl]
            h_heads = []
            for e in range(2):
                hd = 2 * pr + e
                own = lane_lo if e == 0 else jnp.logical_not(lane_lo)
                den_lane = MLSTM_HEAD_DIM if e == 0 else 0
                q_h = jnp.where(own, q_pair, 0.0).astype(BF16)
                k_h = jnp.where(own, k_pair, 0.0).astype(BF16)
                v_aug = jnp.where(own, v_pair, jnp.where(lane == den_lane, 1.0, 0.0))
                bc = b_col[:, MLSTM_HEADS + hd:MLSTM_HEADS + hd + 1]
                lic = gcol[:, hd:hd + 1]
                br = b_row[MLSTM_HEADS + hd:MLSTM_HEADS + hd + 1, :]
                lir = grow[hd:hd + 1, :]
                dm = jnp.where(causal, bc + (lir - br), NEG_INF)
                m_prev = m_ref[bb, hd][:, 0:1]
                m_inter = bc + m_prev
                mt = jnp.maximum(m_inter, jnp.max(dm, axis=-1, keepdims=True))
                s = _dot_nt(q_h, k_h) * jnp.exp(dm - mt)
                decay = jnp.exp(m_inter - mt)
                num = (_dot(s.astype(BF16), v_aug.astype(BF16))
                       + decay * _dot(q_h, ct_ref[bb, hd].astype(BF16)))
                den = num[:, den_lane:den_lane + 1]
                h_heads.append(num / jnp.maximum(jnp.abs(den), jnp.exp(-mt)))
                b_last = bc[T - 1:T, :]
                g_col = b_last - bc + lic
                m_new = jnp.maximum(b_last + m_prev, jnp.max(g_col, axis=0, keepdims=True))
                wg = jnp.exp(g_col - m_new)
                cd = jnp.exp(b_last + m_prev - m_new)
                ct_ref[bb, hd] = cd * ct_ref[bb, hd] + _dot_tn(k_h, (v_aug * wg).astype(BF16))
                m_ref[bb, hd] = jnp.broadcast_to(m_new, (1, LANE))
            h_pair = jnp.where(lane_lo, h_heads[0], h_heads[1])
            sq = h_pair * h_pair
            ms = jnp.where(lane_lo,
                           jnp.sum(jnp.where(lane_lo, sq, 0.0), axis=-1, keepdims=True),
                           jnp.sum(jnp.where(lane_lo, 0.0, sq), axis=-1, keepdims=True)) / MLSTM_HEAD_DIM
            hn = h_pair * lax.rsqrt(ms + EPS) * mn_ref[:, psl]
            ym.append(hn * jax.nn.sigmoid(mo_c[:, psl]))
        store(yp, jnp.concatenate(ym, axis=-1))

    def first_rows(ref, bb, width):
        real = ref[bb, 0:N_META, :].astype(F32)
        return jnp.concatenate([jnp.zeros((T + HIST - N_META, width), F32), real], axis=0)

    for bb in range(n_batch):
        def store_first(yp, ym, bb=bb):
            yp_ref[bb, 0:N_META, :] = yp[T - N_META:].astype(yp_ref.dtype)
            ym_ref[bb, 0:N_META, :] = ym[T - N_META:].astype(ym_ref.dtype)

        chunk(bb, 0, first_rows(xp_ref, bb, POOL_WIDTH), first_rows(mqk_ref, bb, 2 * MLSTM_WIDTH),
              first_rows(mv_ref, bb, MLSTM_WIDTH)[HIST:], first_rows(mo_ref, bb, MLSTM_WIDTH)[HIST:],
              N_META - T, store_first)

    def body(c, carry):
        r0 = pl.multiple_of(N_META + T * (c - 1), 16)
        ext = pl.ds(pl.multiple_of(r0 - HIST, 16), T + HIST)
        cur = pl.ds(r0, T)

        for bb in range(n_batch):
            def store(yp, ym, bb=bb):
                yp_ref[bb, cur, :] = yp.astype(yp_ref.dtype)
                ym_ref[bb, cur, :] = ym.astype(ym_ref.dtype)

            chunk(bb, c, xp_ref[bb, ext, :].astype(F32), mqk_ref[bb, ext, :].astype(F32),
                  mv_ref[bb, cur, :].astype(F32), mo_ref[bb, cur, :].astype(F32), r0, store)
        return carry

    lax.fori_loop(1, n_chunks, body, 0)


def _seq_mixer(xp, mqk, mv, mo, gcol, grow, pw, ps, cw, cb, gbc, gbr, mn):
    B, L, _ = xp.shape
    n_chunks = gcol.shape[1]
    BB = SEQ_BATCH
    assert B % BB == 0
    seq = lambda c: pl.BlockSpec((BB, L, c), lambda b: (b, 0, 0))
    full = lambda a: pl.BlockSpec(a.shape, lambda b: (0,) * a.ndim)
    return pl.pallas_call(
        functools.partial(_seq_kernel, n_chunks=n_chunks, n_batch=BB),
        grid=(B // BB,),
        in_specs=[seq(POOL_WIDTH), seq(2 * MLSTM_WIDTH), seq(MLSTM_WIDTH), seq(MLSTM_WIDTH),
                  pl.BlockSpec((BB,) + gcol.shape[1:], lambda b: (b, 0, 0, 0)),
                  pl.BlockSpec((BB,) + grow.shape[1:], lambda b: (b, 0, 0, 0)),
                  full(pw), full(ps), full(cw), full(cb), full(gbc), full(gbr), full(mn)],
        out_specs=[seq(POOL_WIDTH), seq(MLSTM_WIDTH)],
        out_shape=[jax.ShapeDtypeStruct((B, L, POOL_WIDTH), BF16),
                   jax.ShapeDtypeStruct((B, L, MLSTM_WIDTH), BF16)],
        scratch_shapes=[pltpu.VMEM((BB, MLSTM_HEADS, LANE, LANE), F32),
                        pltpu.VMEM((BB, MLSTM_HEADS, 1, LANE), F32)],
        compiler_params=_cparams(("parallel",)),
        name="seq_mixer",
    )(xp, mqk, mv, mo, gcol, grow, pw, ps, cw, cb, gbc, gbr, mn)


def _outproj_kernel(h_ref, ya_ref, yp_ref, ym_ref, wa_ref, wp_ref, wm_ref, o_ref):
    o_ref[...] = (h_ref[...] + _dot(ya_ref[...], wa_ref[...]) + _dot(yp_ref[...], wp_ref[...])
                  + _dot(ym_ref[...], wm_ref[...]))


def _outproj(h, ya, yp, ym, wa, wp, wm):
    N, D = h.shape
    T = _pick_tile(N, FLAT_TILE)
    row = lambda c: pl.BlockSpec((T, c), lambda i: (i, 0))
    full = lambda a: pl.BlockSpec(a.shape, lambda i: (0,) * a.ndim)
    return pl.pallas_call(
        _outproj_kernel,
        grid=(N // T,),
        in_specs=[row(D), row(ya.shape[1]), row(yp.shape[1]), row(ym.shape[1]), full(wa), full(wp), full(wm)],
        out_specs=row(D),
        out_shape=jax.ShapeDtypeStruct((N, D), F32),
        compiler_params=_cparams(("parallel",)),
        name="outproj",
    )(h, ya, yp, ym, wa, wp, wm)


def _ffn_kernel(h_ref, g_ref, wg_ref, wu_ref, wd_ref, o_ref, xn_ref):
    j = pl.program_id(1)

    @pl.when(j == 0)
    def _():
        xn_ref[...] = _rms(h_ref[...], g_ref[...]).astype(BF16)
        o_ref[...] = h_ref[...]

    xn = xn_ref[...]
    a = _dot(xn, wg_ref[...])
    u = _dot(xn, wu_ref[...])
    o_ref[...] += _dot((a * jax.nn.sigmoid(a) * u).astype(BF16), wd_ref[...])


def _ffn(h, g, wg, wu, wd):
    N, D = h.shape
    FF = wg.shape[1]
    T = _pick_tile(N, FFN_ROW_TILE)
    TF = FF_TILE_DENSE
    assert FF % TF == 0
    return pl.pallas_call(
        _ffn_kernel,
        grid=(N // T, FF // TF),
        in_specs=[pl.BlockSpec((T, D), lambda i, j: (i, 0)),
                  pl.BlockSpec(g.shape, lambda i, j: (0, 0)),
                  pl.BlockSpec((D, TF), lambda i, j: (0, j)),
                  pl.BlockSpec((D, TF), lambda i, j: (0, j)),
                  pl.BlockSpec((TF, D), lambda i, j: (j, 0))],
        out_specs=pl.BlockSpec((T, D), lambda i, j: (i, 0)),
        out_shape=jax.ShapeDtypeStruct((N, D), F32),
        scratch_shapes=[pltpu.VMEM((T, D), BF16)],
        compiler_params=_cparams(("parallel", "arbitrary")),
        name="dense_ffn",
    )(h, g, wg, wu, wd)


def _router_kernel(h_ref, g_ref, wr_ref, route_ref, cnt_ref, run_ref):
    i = pl.program_id(0)
    T = h_ref.shape[0]

    @pl.when(i == 0)
    def _():
        run_ref[...] = jnp.zeros_like(run_ref)

    xn = _rms(h_ref[...], g_ref[...])
    lane = lax.broadcasted_iota(jnp.int32, (T, LANE), 1).astype(F32)
    logits = jnp.where(lane < N_EXPERTS, _dot_f32(xn, wr_ref[...]), NEG_INF)
    v1 = jnp.max(logits, axis=-1, keepdims=True)
    i1 = jnp.min(jnp.where(logits == v1, lane, float(LANE)), axis=-1, keepdims=True)
    rest = jnp.where(lane == i1, NEG_INF, logits)
    v2 = jnp.max(rest, axis=-1, keepdims=True)
    i2 = jnp.min(jnp.where(rest == v2, lane, float(LANE)), axis=-1, keepdims=True)
    e2 = jnp.exp(v2 - v1)
    g1 = 1.0 / (1.0 + e2)
    g2 = e2 / (1.0 + e2)
    member = jnp.logical_or(lane == i1, lane == i2).astype(F32)
    ri = lax.broadcasted_iota(jnp.int32, (T, T), 0)
    ci = lax.broadcasted_iota(jnp.int32, (T, T), 1)
    strict = jnp.where(ci < ri, 1.0, 0.0).astype(BF16)
    rank = run_ref[0:1, :] + _dot(strict, member.astype(BF16))
    r1 = jnp.sum(jnp.where(lane == i1, rank, 0.0), axis=-1, keepdims=True)
    r2 = jnp.sum(jnp.where(lane == i2, rank, 0.0), axis=-1, keepdims=True)
    run_ref[...] = run_ref[...] + jnp.sum(member, axis=0, keepdims=True)
    cnt_ref[...] = run_ref[...]
    route = jnp.where(lane == 0, i1, jnp.where(lane == 1, i2,
            jnp.where(lane == 2, r1, jnp.where(lane == 3, r2,
            jnp.where(lane == 4, g1, jnp.where(lane == 5, g2, 0.0))))))
    route_ref[...] = route


def _router(h, g, wr):
    N, D = h.shape
    T = _pick_tile(N, FLAT_TILE)
    return pl.pallas_call(
        _router_kernel,
        grid=(N // T,),
        in_specs=[pl.BlockSpec((T, D), lambda i: (i, 0)),
                  pl.BlockSpec(g.shape, lambda i: (0, 0)),
                  pl.BlockSpec(wr.shape, lambda i: (0, 0))],
        out_specs=[pl.BlockSpec((T, LANE), lambda i: (i, 0)),
                   pl.BlockSpec((8, LANE), lambda i: (0, 0))],
        out_shape=[jax.ShapeDtypeStruct((N, LANE), F32), jax.ShapeDtypeStruct((8, LANE), F32)],
        scratch_shapes=[pltpu.VMEM((8, LANE), F32)],
        compiler_params=_cparams(("arbitrary",)),
        name="router",
    )(h, g, wr)


def _dispatch_kernel(dest_ref, h_ref, xs_in_ref, xs_ref, sem):
    del xs_in_ref
    n = h_ref.shape[0]

    def copy(t, slot):
        return pltpu.make_async_copy(h_ref.at[pl.ds(t, 1)],
                                     xs_ref.at[pl.ds(dest_ref[0, 0, 2 * t + slot], 1)], sem)

    def issue(t, c):
        copy(t, 0).start()
        copy(t, 1).start()
        return c

    lax.fori_loop(0, n, issue, 0, unroll=8)

    def drain(t, c):
        copy(t, 0).wait()
        copy(t, 1).wait()
        return c

    lax.fori_loop(0, n, drain, 0, unroll=8)


def _dispatch(dest3, h, xs_init):
    n_tiles = dest3.shape[0]
    return pl.pallas_call(
        _dispatch_kernel,
        grid=(n_tiles,),
        in_specs=[pl.BlockSpec((1, 1, dest3.shape[2]), lambda i: (i, 0, 0), memory_space=pltpu.SMEM),
                  pl.BlockSpec((dest3.shape[2] // 2, h.shape[1]), lambda i: (i, 0)),
                  pl.BlockSpec(memory_space=pl.ANY)],
        out_specs=pl.BlockSpec(memory_space=pl.ANY),
        out_shape=jax.ShapeDtypeStruct(xs_init.shape, xs_init.dtype),
        scratch_shapes=[pltpu.SemaphoreType.DMA(())],
        input_output_aliases={2: 0},
        compiler_params=_cparams(("arbitrary",)),
        name="dispatch",
    )(dest3, h, xs_init)


def _expert_kernel(be_ref, x_ref, g_ref, wg_ref, wu_ref, wd_ref, o_ref, xn_ref):
    r = pl.program_id(0)
    j = pl.program_id(1)
    n_active = be_ref[be_ref.shape[0] - 1]

    @pl.when(r < n_active)
    def _():
        @pl.when(j == 0)
        def _():
            xn_ref[...] = _rms(x_ref[...], g_ref[...]).astype(BF16)

        xn = xn_ref[...]
        a = _dot(xn, wg_ref[0])
        u = _dot(xn, wu_ref[0])
        y = _dot((a * jax.nn.sigmoid(a) * u).astype(BF16), wd_ref[0])

        @pl.when(j == 0)
        def _():
            o_ref[...] = y

        @pl.when(j > 0)
        def _():
            o_ref[...] += y

    @pl.when(jnp.logical_and(r >= n_active, j == 0))
    def _():
        o_ref[...] = jnp.zeros_like(o_ref)


def _experts(block_exp, xs, g, wg, wu, wd):
    R, D = xs.shape
    TM = MOE_TILE
    FF = wg.shape[2]
    TF = FF_TILE_EXPERT
    assert FF % TF == 0 and R % TM == 0
    grid_spec = pltpu.PrefetchScalarGridSpec(
        num_scalar_prefetch=1,
        grid=(R // TM, FF // TF),
        in_specs=[pl.BlockSpec((TM, D), lambda r, j, be: (r, 0)),
                  pl.BlockSpec(g.shape, lambda r, j, be: (0, 0)),
                  pl.BlockSpec((1, D, TF), lambda r, j, be: (be[r], 0, j)),
                  pl.BlockSpec((1, D, TF), lambda r, j, be: (be[r], 0, j)),
                  pl.BlockSpec((1, TF, D), lambda r, j, be: (be[r], j, 0))],
        out_specs=pl.BlockSpec((TM, D), lambda r, j, be: (r, 0)),
        scratch_shapes=[pltpu.VMEM((TM, D), BF16)],
    )
    return pl.pallas_call(
        _expert_kernel,
        grid_spec=grid_spec,
        out_shape=jax.ShapeDtypeStruct((R, D), F32),
        compiler_params=_cparams(("parallel", "arbitrary")),
        name="experts",
    )(block_exp, xs, g, wg, wu, wd)


def _combine_kernel(dest_ref, h_ref, route_ref, ys_ref, o_ref, buf_ref, sem):
    T = h_ref.shape[0]

    def copy(t, slot):
        return pltpu.make_async_copy(ys_ref.at[pl.ds(dest_ref[0, 0, 2 * t + slot], 1)],
                                     buf_ref.at[slot, pl.ds(t, 1)], sem)

    def issue(t, c):
        copy(t, 0).start()
        copy(t, 1).start()
        return c

    lax.fori_loop(0, T, issue, 0, unroll=8)

    def drain(t, c):
        copy(t, 0).wait()
        copy(t, 1).wait()
        return c

    lax.fori_loop(0, T, drain, 0, unroll=8)
    route = route_ref[...]
    o_ref[...] = h_ref[...] + route[:, 4:5] * buf_ref[0] + route[:, 5:6] * buf_ref[1]


def _combine(dest3, h, route, ys):
    N, D = h.shape
    T = dest3.shape[2] // 2
    return pl.pallas_call(
        _combine_kernel,
        grid=(N // T,),
        in_specs=[pl.BlockSpec((1, 1, 2 * T), lambda i: (i, 0, 0), memory_space=pltpu.SMEM),
                  pl.BlockSpec((T, D), lambda i: (i, 0)),
                  pl.BlockSpec((T, LANE), lambda i: (i, 0)),
                  pl.BlockSpec(memory_space=pl.ANY)],
        out_specs=pl.BlockSpec((T, D), lambda i: (i, 0)),
        out_shape=jax.ShapeDtypeStruct((N, D), F32),
        scratch_shapes=[pltpu.VMEM((2, T, D), F32), pltpu.SemaphoreType.DMA(())],
        compiler_params=_cparams(("arbitrary",)),
        name="combine",
    )(dest3, h, route, ys)


def _moe(h, g, wr, wg, wu, wd):
    N, D = h.shape
    T = _pick_tile(N, FLAT_TILE)
    route, counts = _router(h, g, wr)
    cnt = counts[0, :N_EXPERTS].astype(jnp.int32)
    padded = (cnt + MOE_TILE - 1) // MOE_TILE * MOE_TILE
    pend = jnp.cumsum(padded)
    pstart = pend - padded
    idx = route[:, 0:2].astype(jnp.int32)
    dest = pstart[idx] + route[:, 2:4].astype(jnp.int32)
    dest3 = dest.reshape(N // T, 1, 2 * T)
    n_blocks = -(-2 * N // MOE_TILE) + N_EXPERTS
    block_exp = jnp.minimum(jnp.searchsorted(pend, jnp.arange(n_blocks) * MOE_TILE, side='right'),
                            N_EXPERTS - 1).astype(jnp.int32)
    block_exp = jnp.concatenate([block_exp, (pend[-1:] // MOE_TILE).astype(jnp.int32)])
    xs = _dispatch(dest3, h, jnp.zeros((n_blocks * MOE_TILE, D), F32))
    ys = _experts(block_exp, xs, g, wg, wu, wd)
    return _combine(dest3, h, route, ys)


def _final_kernel(h_ref, g_ref, o_ref):
    o_ref[0] = _rms(h_ref[0, N_META:, :], g_ref[...])


def _final(h, g):
    B, L, D = h.shape
    return pl.pallas_call(
        _final_kernel,
        grid=(B,),
        in_specs=[pl.BlockSpec((1, L, D), lambda b: (b, 0, 0)), pl.BlockSpec(g.shape, lambda b: (0, 0))],
        out_specs=pl.BlockSpec((1, L - N_META, D), lambda b: (b, 0, 0)),
        out_shape=jax.ShapeDtypeStruct((B, L - N_META, D), F32),
        compiler_params=_cparams(("parallel",)),
        name="final_norm",
    )(h, g)


def _rope_tables(L):
    pos = jnp.arange(L, dtype=jnp.int32)
    inv_freq = ROPE_THETA ** (-jnp.arange(0, QK_ROPE, 2, dtype=F32) / QK_ROPE)
    ang = pos.astype(F32)[:, None] * inv_freq[None, :]
    cos, sin = jnp.cos(ang), jnp.sin(ang)
    z = lambda n: jnp.zeros((L, n), F32)
    scale = (QK_NOPE + QK_ROPE) ** -0.5
    cq = jnp.concatenate([jnp.ones((L, QK_NOPE), F32), cos, cos, z(32)], axis=1) * scale
    s1q = jnp.concatenate([z(QK_NOPE), -sin, z(48)], axis=1) * scale
    s2q = jnp.concatenate([z(QK_NOPE + 16), sin, z(32)], axis=1) * scale
    ck = jnp.concatenate([cos, cos, z(96)], axis=1)
    s1k = jnp.concatenate([-sin, z(112)], axis=1)
    s2k = jnp.concatenate([z(16), sin, z(96)], axis=1)
    return jnp.concatenate([cq, s1q, s2q, ck, s1k, s2k], axis=1)


def _layer_weights(l, w_in, w_q_up, w_kv_up, pool_w, w_out):
    pts = np.cumsum([Q_LORA, KV_LORA, QK_ROPE, POOL_WIDTH, 2 * MLSTM_WIDTH, MLSTM_WIDTH, MLSTM_WIDTH]).tolist()
    wi = w_in[l]
    w_cq, w_ckv, w_kr, w_xp, w_mqk, w_mv, w_mo, w_mg = jnp.split(wi, pts, axis=1)
    small = jnp.concatenate([w_kr, w_mg, jnp.zeros((D_MODEL, LANE - QK_ROPE - 2 * MLSTM_HEADS), F32)], axis=1)
    wbig = jnp.concatenate([w_cq, w_ckv, w_xp, w_mqk, w_mv, w_mo, small], axis=1).astype(BF16)
    wq = w_q_up[l].reshape(Q_LORA, MLA_HEADS, QK_NOPE + QK_ROPE)
    wq = jnp.pad(wq, ((0, 0), (0, 0), (0, HEAD_PAD - QK_NOPE - QK_ROPE))).reshape(Q_LORA, -1).astype(BF16)
    wkv = w_kv_up[l].reshape(KV_LORA, MLA_HEADS, QK_NOPE + V_HEAD)
    wk = jnp.pad(wkv[:, :, :QK_NOPE], ((0, 0), (0, 0), (0, HEAD_PAD - QK_NOPE))).reshape(KV_LORA, -1).astype(BF16)
    wv = wkv[:, :, QK_NOPE:].reshape(KV_LORA, -1).astype(BF16)
    pw = jax.scipy.linalg.block_diag(*[pool_w[l, g] for g in range(POOL_GROUPS)]).astype(BF16)
    wo = w_out[l].astype(BF16)
    n_attn = MLA_HEADS * V_HEAD
    return dict(wbig=wbig, wq=wq, wk=wk, wv=wv, pw=pw,
                wa=wo[:n_attn], wp=wo[n_attn:n_attn + POOL_WIDTH], wm=wo[n_attn + POOL_WIDTH:])


def kernel(x, meta_tokens, attn_norm, w_in, q_norm, w_q_up, kv_norm, w_kv_up, pool_w, pool_scale, conv_w, conv_b, gate_bias, mlstm_norm, w_out, ffn_norm, dense_w_gate, dense_w_up, dense_w_down, router_w, moe_w_gate, moe_w_up, moe_w_down, final_norm):
    B, S, D = x.shape
    L = N_META + S
    N = B * L
    depth = w_in.shape[0]
    n_chunks = 1 + S // SEQ_CHUNK
    assert S % ATTN_TILE == 0 and S % SEQ_CHUNK == 0

    meta = jnp.broadcast_to(meta_tokens[None].astype(x.dtype), (B, N_META, D))
    h = jnp.concatenate([meta, x], axis=1)
    tab = _rope_tables(L)
    row = lambda a: a.reshape(1, -1).astype(F32)

    for l in range(depth):
        lw = _layer_weights(l, w_in, w_q_up, w_kv_up, pool_w, w_out)
        q, k, v, xp, mqk, mv, mo, zs = _inproj(h, row(attn_norm[l]), lw['wbig'], row(q_norm[l]), lw['wq'],
                                               row(kv_norm[l]), lw['wk'], lw['wv'], tab)
        ya = _attention(q, k, v)
        gates = zs[:, :, QK_ROPE:QK_ROPE + 2 * MLSTM_HEADS]
        padv = jnp.concatenate([jnp.full((MLSTM_HEADS,), NEG_INF, F32), jnp.full((MLSTM_HEADS,), 1e4, F32)])
        gpad = jnp.broadcast_to(padv, (B, SEQ_CHUNK - N_META, 2 * MLSTM_HEADS))
        gates = jnp.concatenate([gpad, gates], axis=1).reshape(B, n_chunks, SEQ_CHUNK, 2 * MLSTM_HEADS)
        gcol = jnp.pad(gates, ((0, 0), (0, 0), (0, 0), (0, LANE - 2 * MLSTM_HEADS)))
        grow = gates.transpose(0, 1, 3, 2)
        gbc = jnp.pad(gate_bias[l], (0, LANE - 2 * MLSTM_HEADS)).reshape(1, LANE).astype(F32)
        gbr = gate_bias[l].reshape(2 * MLSTM_HEADS, 1).astype(F32)
        yp, ym = _seq_mixer(xp, mqk, mv, mo, gcol, grow, lw['pw'], row(pool_scale[l]),
                            conv_w[l].astype(F32), row(conv_b[l]), gbc, gbr, row(mlstm_norm[l]))
        hf = _outproj(h.reshape(N, D), ya.reshape(N, -1), yp.reshape(N, -1), ym.reshape(N, -1),
                      lw['wa'], lw['wp'], lw['wm'])
        j = l // 2
        if l % 2 == 0:
            hf = _ffn(hf, row(ffn_norm[l]), dense_w_gate[j].astype(BF16), dense_w_up[j].astype(BF16),
                      dense_w_down[j].astype(BF16))
        else:
            wr = jnp.pad(router_w[j], ((0, 0), (0, LANE - N_EXPERTS))).astype(F32)
            hf = _moe(hf, row(ffn_norm[l]), wr, moe_w_gate[j].astype(BF16), moe_w_up[j].astype(BF16),
                      moe_w_down[j].astype(BF16))
        h = hf.reshape(B, L, D)
    return _final(h, row(final_norm))
```

```python
import functools

import numpy as np
import jax
import jax.numpy as jnp
from jax import lax
from jax.experimental import pallas as pl
from jax.experimental.pallas import tpu as pltpu

F32 = jnp.float32
BF16 = jnp.bfloat16

D_MODEL = 1024
N_META = 16
CHUNK = 64
MLA_HEADS = 8
QK_NOPE = 64
QK_ROPE = 32
V_HEAD = 64
Q_LORA = 768
KV_LORA = 256
ROPE_THETA = 10000.0
POOL_GROUPS = 4
POOL_GROUP_DIM = 64
POOL_WIDTH = 256
MLSTM_HEADS = 4
MLSTM_HEAD_DIM = 64
MLSTM_WIDTH = 256
CONV_WIDTH = 4
D_FF = 2816
N_EXPERTS = 8
D_FF_EXPERT = 3584
EPS = 1e-6
NEG_INF = -1e30

LANE = 128
HEAD_PAD = 128
ATTN_TILE = 256
SEQ_CHUNK = 256
SEQ_TAIL = 128
SEQ_BATCH = 2
FLAT_TILE = 512
FFN_ROW_TILE = 768
MOE_TILE = 1024
FF_TILE_DENSE = 1408
FF_TILE_EXPERT = 896
VMEM_LIMIT = 56 * 1024 * 1024

SEG_CQ = (0, 768)
SEG_CKV = (768, 1024)
SEG_XP = (1024, 1280)
SEG_MQK = (1280, 1792)
SEG_MV = (1792, 2048)
SEG_MO = (2048, 2304)
SEG_SMALL = (2304, 2432)
W_BIG = 2432


def _rms(x, g):
    ms = jnp.mean(x * x, axis=-1, keepdims=True)
    return x * lax.rsqrt(ms + EPS) * g


def _dot(a, b):
    return jnp.dot(a, b, preferred_element_type=F32)


def _dot_nt(a, b):
    return lax.dot_general(a, b, (((1,), (1,)), ((), ())), preferred_element_type=F32)


def _dot_tn(a, b):
    return lax.dot_general(a, b, (((0,), (0,)), ((), ())), preferred_element_type=F32)


def _dot_f32(a, b):
    return jnp.dot(a, b, preferred_element_type=F32, precision=lax.Precision.HIGHEST)


def _log_sigmoid(x):
    return jnp.minimum(x, 0.0) - jnp.log(1.0 + jnp.exp(-jnp.abs(x)))


def _pick_tile(n, target):
    best = None
    for t in range(16, target + 1, 16):
        if n % t == 0:
            best = t
    assert best is not None, (n, target)
    return best


def _cparams(sem):
    return pltpu.CompilerParams(dimension_semantics=sem, vmem_limit_bytes=VMEM_LIMIT)


def _rope_lanes(x, c, s_lo, s_hi):
    return x * c + pltpu.roll(x, LANE - 16, 1) * s_lo + pltpu.roll(x, 16, 1) * s_hi


def _inproj_kernel(h_ref, an_ref, wbig_ref, qn_ref, wq_ref, kvn_ref, wk_ref, wv_ref, tab_ref,
                   q_ref, k_ref, v_ref, xp_ref, mqk_ref, mv_ref, mo_ref, zs_ref):
    xn = _rms(h_ref[0], an_ref[...]).astype(BF16)

    def seg(ab):
        return _dot(xn, wbig_ref[:, ab[0]:ab[1]])

    xp_ref[0] = seg(SEG_XP).astype(BF16)
    mqk_ref[0] = seg(SEG_MQK).astype(BF16)
    mv_ref[0] = seg(SEG_MV).astype(BF16)
    mo_ref[0] = seg(SEG_MO).astype(BF16)
    zs = seg(SEG_SMALL)
    zs_ref[0] = zs

    tab = tab_ref[...]
    cq, s1q, s2q = tab[:, 0:128], tab[:, 128:256], tab[:, 256:384]
    ck, s1k, s2k = tab[:, 384:512], tab[:, 512:640], tab[:, 640:768]

    cqn = _rms(seg(SEG_CQ), qn_ref[...]).astype(BF16)
    qf = _dot(cqn, wq_ref[...])
    for hd in range(MLA_HEADS):
        sl = slice(HEAD_PAD * hd, HEAD_PAD * (hd + 1))
        q_ref[0, hd] = _rope_lanes(qf[:, sl], cq, s1q, s2q).astype(BF16)

    ckvn = _rms(seg(SEG_CKV), kvn_ref[...]).astype(BF16)
    v_ref[0] = _dot(ckvn, wv_ref[...]).astype(BF16)
    kf = _dot(ckvn, wk_ref[...])
    kr = pltpu.roll(_rope_lanes(zs, ck, s1k, s2k), QK_NOPE - 2 * MLSTM_HEADS, 1)
    for hd in range(MLA_HEADS):
        sl = slice(HEAD_PAD * hd, HEAD_PAD * (hd + 1))
        k_ref[0, hd] = (kf[:, sl] + kr).astype(BF16)


def _inproj(h, an, wbig, qn, wq, kvn, wk, wv, tab):
    B, L, D = h.shape
    T = _pick_tile(L, 768)
    nt = L // T
    row = lambda c: pl.BlockSpec((1, T, c), lambda b, i: (b, i, 0))
    full = lambda a: pl.BlockSpec(a.shape, lambda b, i: (0,) * a.ndim)
    widths = (MLA_HEADS * V_HEAD, POOL_WIDTH, 2 * MLSTM_WIDTH, MLSTM_WIDTH, MLSTM_WIDTH)
    head_major = jax.ShapeDtypeStruct((B, MLA_HEADS, L, HEAD_PAD), BF16)
    head_spec = pl.BlockSpec((1, MLA_HEADS, T, HEAD_PAD), lambda b, i: (b, 0, i, 0))
    out_shape = [head_major, head_major] + [jax.ShapeDtypeStruct((B, L, c), BF16) for c in widths]
    out_shape.append(jax.ShapeDtypeStruct((B, L, LANE), F32))
    out_specs = [head_spec, head_spec] + [row(c) for c in widths] + [row(LANE)]
    return pl.pallas_call(
        _inproj_kernel,
        grid=(B, nt),
        in_specs=[row(D), full(an), full(wbig), full(qn), full(wq), full(kvn), full(wk), full(wv),
                  pl.BlockSpec((T, tab.shape[1]), lambda b, i: (i, 0))],
        out_specs=out_specs,
        out_shape=out_shape,
        compiler_params=_cparams(("parallel", "parallel")),
        name="inproj",
    )(h, an, wbig, qn, wq, kvn, wk, wv, tab)


def _attn_kernel(q_ref, k_ref, v_ref, o_ref, vm_ref, s_ref, p_ref, oacc_ref, *, n_tiles):
    A = ATTN_TILE
    lane = lax.broadcasted_iota(jnp.int32, (1, LANE), 1)
    ri = lax.broadcasted_iota(jnp.int32, (A, A), 0) // CHUNK
    ci = lax.broadcasted_iota(jnp.int32, (A, A), 1) // CHUNK
    diag_mask = ci <= ri
    oacc_ref[...] = jnp.zeros_like(oacc_ref)

    def head_body(hh, carry):
        vm_ref[...] = jnp.where(lane // V_HEAD == hh, v_ref[0], jnp.zeros_like(v_ref[0]))

        s0 = _dot_nt(q_ref[0, hh, 0:N_META, :], k_ref[0, hh, 0:N_META, :])
        p0 = jnp.exp(s0 - jnp.max(s0, axis=-1, keepdims=True))
        o0 = _dot(p0.astype(BF16), vm_ref[0:N_META, :]) / jnp.sum(p0, axis=-1, keepdims=True)
        oacc_ref[0:N_META, :] += o0

        for j in range(n_tiles):
            r0 = N_META + A * j
            qj = q_ref[0, hh, r0:r0 + A, :]
            s_meta = _dot_nt(qj, k_ref[0, hh, 0:N_META, :])
            m128 = None
            for c in range(j + 1):
                s = _dot_nt(qj, k_ref[0, hh, N_META + A * c:N_META + A * (c + 1), :])
                if c == j:
                    s = jnp.where(diag_mask, s, NEG_INF)
                s_ref[:, A * c:A * (c + 1)] = s
                mc = jnp.maximum(s[:, :LANE], s[:, LANE:])
                m128 = mc if m128 is None else jnp.maximum(m128, mc)
            m = jnp.maximum(jnp.max(m128, axis=-1, keepdims=True),
                            jnp.max(s_meta, axis=-1, keepdims=True))
            p_meta = jnp.exp(s_meta - m)
            l128 = jnp.zeros((A, LANE), F32)
            for c in range(j + 1):
                p = jnp.exp(s_ref[:, A * c:A * (c + 1)] - m)
                l128 = l128 + p[:, :LANE] + p[:, LANE:]
                p_ref[:, A * c:A * (c + 1)] = p.astype(BF16)
            l = jnp.sum(p_meta, axis=-1, keepdims=True) + jnp.sum(l128, axis=-1, keepdims=True)
            n = A * (j + 1)
            acc = (_dot(p_ref[:, 0:n], vm_ref[N_META:N_META + n, :])
                   + _dot(p_meta.astype(BF16), vm_ref[0:N_META, :]))
            oacc_ref[r0:r0 + A, :] += acc / l
        return carry

    lax.fori_loop(0, 2, head_body, 0)
    o_ref[0] = oacc_ref[...].astype(o_ref.dtype)


def _attention(q, k, v):
    B, _, L, _ = q.shape
    n_tiles = (L - N_META) // ATTN_TILE
    assert N_META + n_tiles * ATTN_TILE == L
    n_pairs = MLA_HEADS // 2
    qk_spec = pl.BlockSpec((1, 2, L, HEAD_PAD), lambda b, p: (b, p, 0, 0))
    v_spec = pl.BlockSpec((1, L, 2 * V_HEAD), lambda b, p: (b, 0, p))
    return pl.pallas_call(
        functools.partial(_attn_kernel, n_tiles=n_tiles),
        grid=(B, n_pairs),
        in_specs=[qk_spec, qk_spec, v_spec],
        out_specs=v_spec,
        out_shape=jax.ShapeDtypeStruct((B, L, MLA_HEADS * V_HEAD), BF16),
        scratch_shapes=[pltpu.VMEM((L, 2 * V_HEAD), BF16),
                        pltpu.VMEM((ATTN_TILE, n_tiles * ATTN_TILE), F32),
                        pltpu.VMEM((ATTN_TILE, n_tiles * ATTN_TILE), BF16),
                        pltpu.VMEM((L, 2 * V_HEAD), F32)],
        compiler_params=_cparams(("parallel", "parallel")),
        name="attention",
    )(q, k, v)


def _seq_kernel(xp_ref, mqk_ref, mv_ref, mo_ref, zs_ref, pw_ref, ps_ref,
                cw_ref, cb_ref, gbc_ref, mn_ref, yp_ref, ym_ref, ct_ref, m_ref, *, n_full, n_batch):
    HIST = 16
    L = xp_ref.shape[1]
    lane = lax.broadcasted_iota(jnp.int32, (1, LANE), 1)
    lane_lo = lane < MLSTM_HEAD_DIM
    lane256 = lax.broadcasted_iota(jnp.int32, (1, POOL_WIDTH), 1)
    grp = lane256 // POOL_GROUP_DIM
    win = jnp.left_shift(2, grp).astype(F32)
    row8 = lax.broadcasted_iota(jnp.int32, (2 * MLSTM_HEADS, 1), 0)

    ct_ref[...] = jnp.zeros_like(ct_ref)
    m_ref[...] = jnp.zeros_like(m_ref)

    def chunk(T, bb, gcol_raw, xx_p, xx_c, mv_c, mo_c, pos0, store, update_state=True):
        ti = lax.broadcasted_iota(jnp.int32, (T, T), 0)
        si = lax.broadcasted_iota(jnp.int32, (T, T), 1)
        causal = si <= ti
        tril = causal.astype(F32)
        triu = (ti <= si).astype(F32)
        t_col = lax.broadcasted_iota(jnp.int32, (T, 1), 0)
        s2 = xx_p + pltpu.roll(xx_p, 1, 0)
        s4 = s2 + pltpu.roll(s2, 2, 0)
        s8 = s4 + pltpu.roll(s4, 4, 0)
        s16 = s8 + pltpu.roll(s8, 8, 0)
        ssum = jnp.where(grp == 0, s2[HIST:], jnp.where(grp == 1, s4[HIST:],
                         jnp.where(grp == 2, s8[HIST:], s16[HIST:])))
        cnt = jnp.maximum(pos0 + t_col + 1, 1).astype(F32)
        d = ssum / jnp.minimum(cnt, win) - xx_p[HIST:]
        yp = _dot(d.astype(BF16), pw_ref[...]) * ps_ref[...]

        cw = cw_ref[...]
        y = (xx_c * cw[3:4] + pltpu.roll(xx_c, 1, 0) * cw[2:3] + pltpu.roll(xx_c, 2, 0) * cw[1:2]
             + pltpu.roll(xx_c, 3, 0) * cw[0:1])[HIST:] + cb_ref[...]
        qk = y * jax.nn.sigmoid(y)
        qc = qk[:, :MLSTM_WIDTH]
        kc = qk[:, MLSTM_WIDTH:] * (MLSTM_HEAD_DIM ** -0.5)

        gcol = gcol_raw + gbc_ref[...]
        grow = gcol.T[0:2 * MLSTM_HEADS, :]
        is_f = jnp.logical_and(lane >= MLSTM_HEADS, lane < 2 * MLSTM_HEADS)
        b_col = _dot_f32(tril, jnp.where(is_f, _log_sigmoid(gcol), 0.0))
        b_row = _dot_f32(jnp.where(row8 >= MLSTM_HEADS, _log_sigmoid(grow), 0.0), triu)

        ym = []
        for pr in range(MLSTM_HEADS // 2):
            psl = slice(LANE * pr, LANE * (pr + 1))
            q_pair, k_pair = qc[:, psl], kc[:, psl]
            v_pair = mv_c[:, psl]
            h_heads = []
            for e in range(2):
                hd = 2 * pr + e
                own = lane_lo if e == 0 else jnp.logical_not(lane_lo)
                den_lane = MLSTM_HEAD_DIM if e == 0 else 0
                q_h = jnp.where(own, q_pair, 0.0).astype(BF16)
                k_h = jnp.where(own, k_pair, 0.0).astype(BF16)
                v_aug = jnp.where(own, v_pair, jnp.where(lane == den_lane, 1.0, 0.0))
                bc = b_col[:, MLSTM_HEADS + hd:MLSTM_HEADS + hd + 1]
                lic = gcol[:, hd:hd + 1]
                br = b_row[MLSTM_HEADS + hd:MLSTM_HEADS + hd + 1, :]
                lir = grow[hd:hd + 1, :]
                dm = jnp.where(causal, bc + (lir - br), NEG_INF)
                m_prev = m_ref[bb, hd][:, 0:1]
                m_inter = bc + m_prev
                mt = jnp.maximum(m_inter, jnp.max(dm, axis=-1, keepdims=True))
                s = _dot_nt(q_h, k_h) * jnp.exp(dm - mt)
                decay = jnp.exp(m_inter - mt)
                num = (_dot(s.astype(BF16), v_aug.astype(BF16))
                       + decay * _dot(q_h, ct_ref[bb, hd].astype(BF16)))
                den = num[:, den_lane:den_lane + 1]
                h_heads.append(num / jnp.maximum(jnp.abs(den), jnp.exp(-mt)))
                if not update_state:
                    continue
                b_last = bc[T - 1:T, :]
                g_col = b_last - bc + lic
                m_new = jnp.maximum(b_last + m_prev, jnp.max(g_col, axis=0, keepdims=True))
                wg = jnp.exp(g_col - m_new)
                cd = jnp.exp(b_last + m_prev - m_new)
                ct_ref[bb, hd] = cd * ct_ref[bb, hd] + _dot_tn(k_h, (v_aug * wg).astype(BF16))
                m_ref[bb, hd] = jnp.broadcast_to(m_new, (1, LANE))
            h_pair = jnp.where(lane_lo, h_heads[0], h_heads[1])
            sq = h_pair * h_pair
            ms = jnp.where(lane_lo,
                           jnp.sum(jnp.where(lane_lo, sq, 0.0), axis=-1, keepdims=True),
                           jnp.sum(jnp.where(lane_lo, 0.0, sq), axis=-1, keepdims=True)) / MLSTM_HEAD_DIM
            hn = h_pair * lax.rsqrt(ms + EPS) * mn_ref[:, psl]
            ym.append(hn * jax.nn.sigmoid(mo_c[:, psl]))
        store(yp, jnp.concatenate(ym, axis=-1))

    T = SEQ_CHUNK
    TT = SEQ_TAIL
    tail = L - n_full * T

    def store_rows(bb, rows, n=None):
        def store(yp, ym):
            yp_ref[bb, rows, :] = (yp if n is None else yp[:n]).astype(yp_ref.dtype)
            ym_ref[bb, rows, :] = (ym if n is None else ym[:n]).astype(ym_ref.dtype)
        return store

    def head(ref, bb, width):
        return jnp.concatenate([jnp.zeros((HIST, width), F32), ref[bb, 0:T, :].astype(F32)], axis=0)

    for bb in range(n_batch):
        chunk(T, bb, zs_ref[bb, 0:T, :], head(xp_ref, bb, POOL_WIDTH),
              head(mqk_ref, bb, 2 * MLSTM_WIDTH), mv_ref[bb, 0:T, :].astype(F32),
              mo_ref[bb, 0:T, :].astype(F32), 0, store_rows(bb, slice(0, T)))

    def body(c, carry):
        r0 = pl.multiple_of(T * c, T)
        ext = pl.ds(pl.multiple_of(r0 - HIST, 16), T + HIST)
        cur = pl.ds(r0, T)
        for bb in range(n_batch):
            chunk(T, bb, zs_ref[bb, cur, :], xp_ref[bb, ext, :].astype(F32),
                  mqk_ref[bb, ext, :].astype(F32), mv_ref[bb, cur, :].astype(F32),
                  mo_ref[bb, cur, :].astype(F32), r0, store_rows(bb, cur))
        return carry

    lax.fori_loop(1, n_full, body, 0)

    r0 = n_full * T

    def padded(ref, bb, width, lo):
        return jnp.concatenate([ref[bb, lo:L, :].astype(F32), jnp.zeros((TT - tail, width), F32)], axis=0)

    for bb in range(n_batch):
        closed = jnp.broadcast_to(jnp.where(lane < MLSTM_HEADS, NEG_INF, 1e4), (TT - tail, LANE))
        chunk(TT, bb, jnp.concatenate([zs_ref[bb, r0:L, :], closed], axis=0),
              padded(xp_ref, bb, POOL_WIDTH, r0 - HIST),
              padded(mqk_ref, bb, 2 * MLSTM_WIDTH, r0 - HIST), padded(mv_ref, bb, MLSTM_WIDTH, r0),
              padded(mo_ref, bb, MLSTM_WIDTH, r0), r0, store_rows(bb, slice(r0, L), tail),
              update_state=False)


def _seq_mixer(xp, mqk, mv, mo, zs, pw, ps, cw, cb, gbc, mn):
    B, L, _ = xp.shape
    n_full = L // SEQ_CHUNK
    assert 0 < L - n_full * SEQ_CHUNK <= SEQ_TAIL
    BB = SEQ_BATCH
    assert B % BB == 0
    seq = lambda c: pl.BlockSpec((BB, L, c), lambda b: (b, 0, 0))
    full = lambda a: pl.BlockSpec(a.shape, lambda b: (0,) * a.ndim)
    return pl.pallas_call(
        functools.partial(_seq_kernel, n_full=n_full, n_batch=BB),
        grid=(B // BB,),
        in_specs=[seq(POOL_WIDTH), seq(2 * MLSTM_WIDTH), seq(MLSTM_WIDTH), seq(MLSTM_WIDTH),
                  seq(LANE), full(pw), full(ps), full(cw), full(cb), full(gbc), full(mn)],
        out_specs=[seq(POOL_WIDTH), seq(MLSTM_WIDTH)],
        out_shape=[jax.ShapeDtypeStruct((B, L, POOL_WIDTH), BF16),
                   jax.ShapeDtypeStruct((B, L, MLSTM_WIDTH), BF16)],
        scratch_shapes=[pltpu.VMEM((BB, MLSTM_HEADS, LANE, LANE), F32),
                        pltpu.VMEM((BB, MLSTM_HEADS, 1, LANE), F32)],
        compiler_params=_cparams(("parallel",)),
        name="seq_mixer",
    )(xp, mqk, mv, mo, zs, pw, ps, cw, cb, gbc, mn)


def _ffn_kernel(h_ref, ya_ref, yp_ref, ym_ref, wa_ref, wp_ref, wm_ref, g_ref, wg_ref, wu_ref, wd_ref,
                o_ref, xn_ref):
    j = pl.program_id(1)

    @pl.when(j == 0)
    def _():
        hf = (h_ref[...] + _dot(ya_ref[...], wa_ref[...]) + _dot(yp_ref[...], wp_ref[...])
              + _dot(ym_ref[...], wm_ref[...]))
        xn_ref[...] = _rms(hf, g_ref[...]).astype(BF16)
        o_ref[...] = hf

    xn = xn_ref[...]
    a = _dot(xn, wg_ref[...])
    u = _dot(xn, wu_ref[...])
    o_ref[...] += _dot((a * jax.nn.sigmoid(a) * u).astype(BF16), wd_ref[...])


def _ffn(h, ya, yp, ym, wa, wp, wm, g, wg, wu, wd):
    N, D = h.shape
    FF = wg.shape[1]
    T = _pick_tile(N, FFN_ROW_TILE)
    TF = FF_TILE_DENSE
    assert FF % TF == 0
    row = lambda a: pl.BlockSpec((T, a.shape[1]), lambda i, j: (i, 0))
    full = lambda a: pl.BlockSpec(a.shape, lambda i, j: (0, 0))
    return pl.pallas_call(
        _ffn_kernel,
        grid=(N // T, FF // TF),
        in_specs=[row(h), row(ya), row(yp), row(ym), full(wa), full(wp), full(wm), full(g),
                  pl.BlockSpec((D, TF), lambda i, j: (0, j)),
                  pl.BlockSpec((D, TF), lambda i, j: (0, j)),
                  pl.BlockSpec((TF, D), lambda i, j: (j, 0))],
        out_specs=pl.BlockSpec((T, D), lambda i, j: (i, 0)),
        out_shape=jax.ShapeDtypeStruct((N, D), F32),
        scratch_shapes=[pltpu.VMEM((T, D), BF16)],
        compiler_params=_cparams(("parallel", "arbitrary")),
        name="dense_ffn",
    )(h, ya, yp, ym, wa, wp, wm, g, wg, wu, wd)


def _router_kernel(h_ref, ya_ref, yp_ref, ym_ref, wa_ref, wp_ref, wm_ref, g_ref, wr_ref, tri_ref,
                   hf_ref, route_ref, cnt_ref, run_ref):
    i = pl.program_id(0)
    T = h_ref.shape[0]

    @pl.when(i == 0)
    def _():
        run_ref[...] = jnp.zeros_like(run_ref)

    hf = (h_ref[...] + _dot(ya_ref[...], wa_ref[...]) + _dot(yp_ref[...], wp_ref[...])
          + _dot(ym_ref[...], wm_ref[...]))
    hf_ref[...] = hf
    xn = _rms(hf, g_ref[...])
    lane = lax.broadcasted_iota(jnp.int32, (T, LANE), 1).astype(F32)
    logits = jnp.where(lane < N_EXPERTS, _dot_f32(xn, wr_ref[...]), NEG_INF)
    v1 = jnp.max(logits, axis=-1, keepdims=True)
    i1 = jnp.min(jnp.where(logits == v1, lane, float(LANE)), axis=-1, keepdims=True)
    rest = jnp.where(lane == i1, NEG_INF, logits)
    v2 = jnp.max(rest, axis=-1, keepdims=True)
    i2 = jnp.min(jnp.where(rest == v2, lane, float(LANE)), axis=-1, keepdims=True)
    e2 = jnp.exp(v2 - v1)
    g1 = 1.0 / (1.0 + e2)
    g2 = e2 / (1.0 + e2)
    member = jnp.logical_or(lane == i1, lane == i2).astype(F32)
    rank = run_ref[0:1, :] + _dot(tri_ref[...], member.astype(BF16))
    r1 = jnp.sum(jnp.where(lane == i1, rank, 0.0), axis=-1, keepdims=True)
    r2 = jnp.sum(jnp.where(lane == i2, rank, 0.0), axis=-1, keepdims=True)
    run_ref[...] = run_ref[...] + jnp.sum(member, axis=0, keepdims=True)
    cnt_ref[...] = run_ref[...]
    route = jnp.where(lane == 0, i1, jnp.where(lane == 1, i2,
            jnp.where(lane == 2, r1, jnp.where(lane == 3, r2,
            jnp.where(lane == 4, g1, jnp.where(lane == 5, g2, 0.0))))))
    route_ref[...] = route


def _router(h, ya, yp, ym, wa, wp, wm, g, wr):
    N, D = h.shape
    T = _pick_tile(N, FLAT_TILE)
    tri = jnp.tril(jnp.ones((T, T), BF16), -1)
    row = lambda a: pl.BlockSpec((T, a.shape[1]), lambda i: (i, 0))
    full = lambda a: pl.BlockSpec(a.shape, lambda i: (0, 0))
    return pl.pallas_call(
        _router_kernel,
        grid=(N // T,),
        in_specs=[row(h), row(ya), row(yp), row(ym), full(wa), full(wp), full(wm), full(g), full(wr), full(tri)],
        out_specs=[pl.BlockSpec((T, D), lambda i: (i, 0)),
                   pl.BlockSpec((T, LANE), lambda i: (i, 0)),
                   pl.BlockSpec((8, LANE), lambda i: (0, 0))],
        out_shape=[jax.ShapeDtypeStruct((N, D), F32), jax.ShapeDtypeStruct((N, LANE), F32),
                   jax.ShapeDtypeStruct((8, LANE), F32)],
        scratch_shapes=[pltpu.VMEM((8, LANE), F32)],
        compiler_params=_cparams(("arbitrary",)),
        name="router",
    )(h, ya, yp, ym, wa, wp, wm, g, wr, tri)


def _dispatch_kernel(dest_ref, h_ref, xs_in_ref, xs_ref, sem):
    del xs_in_ref
    n = h_ref.shape[0]

    def copy(t, slot):
        return pltpu.make_async_copy(h_ref.at[pl.ds(t, 1)],
                                     xs_ref.at[pl.ds(dest_ref[0, 0, 2 * t + slot], 1)], sem)

    def issue(t, c):
        copy(t, 0).start()
        copy(t, 1).start()
        return c

    lax.fori_loop(0, n, issue, 0, unroll=8)

    def drain(t, c):
        copy(t, 0).wait()
        copy(t, 1).wait()
        return c

    lax.fori_loop(0, n, drain, 0, unroll=8)


def _dispatch(dest3, h, xs_init):
    n_tiles = dest3.shape[0]
    return pl.pallas_call(
        _dispatch_kernel,
        grid=(n_tiles,),
        in_specs=[pl.BlockSpec((1, 1, dest3.shape[2]), lambda i: (i, 0, 0), memory_space=pltpu.SMEM),
                  pl.BlockSpec((dest3.shape[2] // 2, h.shape[1]), lambda i: (i, 0)),
                  pl.BlockSpec(memory_space=pl.ANY)],
        out_specs=pl.BlockSpec(memory_space=pl.ANY),
        out_shape=jax.ShapeDtypeStruct(xs_init.shape, xs_init.dtype),
        scratch_shapes=[pltpu.SemaphoreType.DMA(())],
        input_output_aliases={2: 0},
        compiler_params=_cparams(("arbitrary",)),
        name="dispatch",
    )(dest3, h, xs_init)


def _expert_kernel(be_ref, x_ref, g_ref, wg_ref, wu_ref, wd_ref, o_ref, xn_ref):
    r = pl.program_id(0)
    j = pl.program_id(1)
    n_active = be_ref[be_ref.shape[0] - 1]

    @pl.when(r < n_active)
    def _():
        @pl.when(j == 0)
        def _():
            xn_ref[...] = _rms(x_ref[...], g_ref[...]).astype(BF16)

        xn = xn_ref[...]
        a = _dot(xn, wg_ref[0])
        u = _dot(xn, wu_ref[0])
        y = _dot((a * jax.nn.sigmoid(a) * u).astype(BF16), wd_ref[0])

        @pl.when(j == 0)
        def _():
            o_ref[...] = y

        @pl.when(j > 0)
        def _():
            o_ref[...] += y

    @pl.when(jnp.logical_and(r >= n_active, j == 0))
    def _():
        o_ref[...] = jnp.zeros_like(o_ref)


def _experts(block_exp, xs, g, wg, wu, wd):
    R, D = xs.shape
    TM = MOE_TILE
    FF = wg.shape[2]
    TF = FF_TILE_EXPERT
    assert FF % TF == 0 and R % TM == 0
    grid_spec = pltpu.PrefetchScalarGridSpec(
        num_scalar_prefetch=1,
        grid=(R // TM, FF // TF),
        in_specs=[pl.BlockSpec((TM, D), lambda r, j, be: (r, 0)),
                  pl.BlockSpec(g.shape, lambda r, j, be: (0, 0)),
                  pl.BlockSpec((1, D, TF), lambda r, j, be: (be[r], 0, j)),
                  pl.BlockSpec((1, D, TF), lambda r, j, be: (be[r], 0, j)),
                  pl.BlockSpec((1, TF, D), lambda r, j, be: (be[r], j, 0))],
        out_specs=pl.BlockSpec((TM, D), lambda r, j, be: (r, 0)),
        scratch_shapes=[pltpu.VMEM((TM, D), BF16)],
    )
    return pl.pallas_call(
        _expert_kernel,
        grid_spec=grid_spec,
        out_shape=jax.ShapeDtypeStruct((R, D), F32),
        compiler_params=_cparams(("parallel", "arbitrary")),
        name="experts",
    )(block_exp, xs, g, wg, wu, wd)


def _combine_kernel(dest_ref, h_ref, route_ref, ys_ref, o_ref, buf_ref, sem):
    T = h_ref.shape[0]

    def copy(t, slot):
        return pltpu.make_async_copy(ys_ref.at[pl.ds(dest_ref[0, 0, 2 * t + slot], 1)],
                                     buf_ref.at[slot, pl.ds(t, 1)], sem)

    def issue(t, c):
        copy(t, 0).start()
        copy(t, 1).start()
        return c

    lax.fori_loop(0, T, issue, 0, unroll=8)

    def drain(t, c):
        copy(t, 0).wait()
        copy(t, 1).wait()
        return c

    lax.fori_loop(0, T, drain, 0, unroll=8)
    route = route_ref[...]
    o_ref[...] = h_ref[...] + route[:, 4:5] * buf_ref[0] + route[:, 5:6] * buf_ref[1]


def _combine(dest3, h, route, ys):
    N, D = h.shape
    T = dest3.shape[2] // 2
    return pl.pallas_call(
        _combine_kernel,
        grid=(N // T,),
        in_specs=[pl.BlockSpec((1, 1, 2 * T), lambda i: (i, 0, 0), memory_space=pltpu.SMEM),
                  pl.BlockSpec((T, D), lambda i: (i, 0)),
                  pl.BlockSpec((T, LANE), lambda i: (i, 0)),
                  pl.BlockSpec(memory_space=pl.ANY)],
        out_specs=pl.BlockSpec((T, D), lambda i: (i, 0)),
        out_shape=jax.ShapeDtypeStruct((N, D), F32),
        scratch_shapes=[pltpu.VMEM((2, T, D), F32), pltpu.SemaphoreType.DMA(())],
        compiler_params=_cparams(("arbitrary",)),
        name="combine",
    )(dest3, h, route, ys)


def _moe(mixed, g, wr, wg, wu, wd):
    N, D = mixed[0].shape
    T = _pick_tile(N, FLAT_TILE)
    h, route, counts = _router(*mixed, g, wr)
    cnt = counts[0, :N_EXPERTS].astype(jnp.int32)
    padded = (cnt + MOE_TILE - 1) // MOE_TILE * MOE_TILE
    pend = jnp.cumsum(padded)
    pstart = pend - padded
    idx = route[:, 0:2].astype(jnp.int32)
    dest = pstart[idx] + route[:, 2:4].astype(jnp.int32)
    dest3 = dest.reshape(N // T, 1, 2 * T)
    n_blocks = -(-2 * N // MOE_TILE) + N_EXPERTS
    block_exp = jnp.minimum(jnp.searchsorted(pend, jnp.arange(n_blocks) * MOE_TILE, side='right'),
                            N_EXPERTS - 1).astype(jnp.int32)
    block_exp = jnp.concatenate([block_exp, (pend[-1:] // MOE_TILE).astype(jnp.int32)])
    xs = _dispatch(dest3, h, jnp.zeros((n_blocks * MOE_TILE, D), F32))
    ys = _experts(block_exp, xs, g, wg, wu, wd)
    return _combine(dest3, h, route, ys)


def _final_kernel(h_ref, g_ref, o_ref):
    o_ref[0] = _rms(h_ref[0, N_META:, :], g_ref[...])


def _final(h, g):
    B, L, D = h.shape
    return pl.pallas_call(
        _final_kernel,
        grid=(B,),
        in_specs=[pl.BlockSpec((1, L, D), lambda b: (b, 0, 0)), pl.BlockSpec(g.shape, lambda b: (0, 0))],
        out_specs=pl.BlockSpec((1, L - N_META, D), lambda b: (b, 0, 0)),
        out_shape=jax.ShapeDtypeStruct((B, L - N_META, D), F32),
        compiler_params=_cparams(("parallel",)),
        name="final_norm",
    )(h, g)


def _rope_tables(L):
    pos = jnp.arange(L, dtype=jnp.int32)
    inv_freq = ROPE_THETA ** (-jnp.arange(0, QK_ROPE, 2, dtype=F32) / QK_ROPE)
    ang = pos.astype(F32)[:, None] * inv_freq[None, :]
    cos, sin = jnp.cos(ang), jnp.sin(ang)
    z = lambda n: jnp.zeros((L, n), F32)
    scale = (QK_NOPE + QK_ROPE) ** -0.5
    cq = jnp.concatenate([jnp.ones((L, QK_NOPE), F32), cos, cos, z(32)], axis=1) * scale
    s1q = jnp.concatenate([z(QK_NOPE), -sin, z(48)], axis=1) * scale
    s2q = jnp.concatenate([z(QK_NOPE + 16), sin, z(32)], axis=1) * scale
    g = 2 * MLSTM_HEADS
    ck = jnp.concatenate([z(g), cos, cos, z(96 - g)], axis=1)
    s1k = jnp.concatenate([z(g), -sin, z(112 - g)], axis=1)
    s2k = jnp.concatenate([z(g + 16), sin, z(96 - g)], axis=1)
    return jnp.concatenate([cq, s1q, s2q, ck, s1k, s2k], axis=1)


def _layer_weights(l, w_in, w_q_up, w_kv_up, pool_w, w_out):
    pts = np.cumsum([Q_LORA, KV_LORA, QK_ROPE, POOL_WIDTH, 2 * MLSTM_WIDTH, MLSTM_WIDTH, MLSTM_WIDTH]).tolist()
    wi = w_in[l]
    w_cq, w_ckv, w_kr, w_xp, w_mqk, w_mv, w_mo, w_mg = jnp.split(wi, pts, axis=1)
    small = jnp.concatenate([w_mg, w_kr, jnp.zeros((D_MODEL, LANE - QK_ROPE - 2 * MLSTM_HEADS), F32)], axis=1)
    wbig = jnp.concatenate([w_cq, w_ckv, w_xp, w_mqk, w_mv, w_mo, small], axis=1).astype(BF16)
    wq = w_q_up[l].reshape(Q_LORA, MLA_HEADS, QK_NOPE + QK_ROPE)
    wq = jnp.pad(wq, ((0, 0), (0, 0), (0, HEAD_PAD - QK_NOPE - QK_ROPE))).reshape(Q_LORA, -1).astype(BF16)
    wkv = w_kv_up[l].reshape(KV_LORA, MLA_HEADS, QK_NOPE + V_HEAD)
    wk = jnp.pad(wkv[:, :, :QK_NOPE], ((0, 0), (0, 0), (0, HEAD_PAD - QK_NOPE))).reshape(KV_LORA, -1).astype(BF16)
    wv = wkv[:, :, QK_NOPE:].reshape(KV_LORA, -1).astype(BF16)
    pw = jax.scipy.linalg.block_diag(*[pool_w[l, g] for g in range(POOL_GROUPS)]).astype(BF16)
    wo = w_out[l].astype(BF16)
    n_attn = MLA_HEADS * V_HEAD
    return dict(wbig=wbig, wq=wq, wk=wk, wv=wv, pw=pw,
                wa=wo[:n_attn], wp=wo[n_attn:n_attn + POOL_WIDTH], wm=wo[n_attn + POOL_WIDTH:])


def kernel(x, meta_tokens, attn_norm, w_in, q_norm, w_q_up, kv_norm, w_kv_up, pool_w, pool_scale, conv_w, conv_b, gate_bias, mlstm_norm, w_out, ffn_norm, dense_w_gate, dense_w_up, dense_w_down, router_w, moe_w_gate, moe_w_up, moe_w_down, final_norm):
    B, S, D = x.shape
    L = N_META + S
    N = B * L
    depth = w_in.shape[0]
    n_full = L // SEQ_CHUNK
    assert S % ATTN_TILE == 0 and S % SEQ_CHUNK == 0

    meta = jnp.broadcast_to(meta_tokens[None].astype(x.dtype), (B, N_META, D))
    h = jnp.concatenate([meta, x], axis=1)
    tab = _rope_tables(L)
    row = lambda a: a.reshape(1, -1).astype(F32)

    for l in range(depth):
        lw = _layer_weights(l, w_in, w_q_up, w_kv_up, pool_w, w_out)
        q, k, v, xp, mqk, mv, mo, zs = _inproj(h, row(attn_norm[l]), lw['wbig'], row(q_norm[l]), lw['wq'],
                                               row(kv_norm[l]), lw['wk'], lw['wv'], tab)
        ya = _attention(q, k, v)
        gbc = jnp.pad(gate_bias[l], (0, LANE - 2 * MLSTM_HEADS)).reshape(1, LANE).astype(F32)
        yp, ym = _seq_mixer(xp, mqk, mv, mo, zs, lw['pw'], row(pool_scale[l]),
                            conv_w[l].astype(F32), row(conv_b[l]), gbc, row(mlstm_norm[l]))
        mixed = (h.reshape(N, D), ya.reshape(N, -1), yp.reshape(N, -1), ym.reshape(N, -1),
                 lw['wa'], lw['wp'], lw['wm'])
        j = l // 2
        if l % 2 == 0:
            hf = _ffn(*mixed, row(ffn_norm[l]), dense_w_gate[j].astype(BF16), dense_w_up[j].astype(BF16),
                      dense_w_down[j].astype(BF16))
        else:
            wr = jnp.pad(router_w[j], ((0, 0), (0, LANE - N_EXPERTS))).astype(F32)
            hf = _moe(mixed, row(ffn_norm[l]), wr, moe_w_gate[j].astype(BF16), moe_w_up[j].astype(BF16),
                      moe_w_down[j].astype(BF16))
        h = hf.reshape(B, L, D)
    return _final(h, row(final_norm))
```

```python
import functools

import numpy as np
import jax
import jax.numpy as jnp
from jax import lax
from jax.experimental import pallas as pl
from jax.experimental.pallas import tpu as pltpu

F32 = jnp.float32
BF16 = jnp.bfloat16

D_MODEL = 1024
N_META = 16
CHUNK = 64
MLA_HEADS = 8
QK_NOPE = 64
QK_ROPE = 32
V_HEAD = 64
Q_LORA = 768
KV_LORA = 256
ROPE_THETA = 10000.0
POOL_GROUPS = 4
POOL_GROUP_DIM = 64
POOL_WIDTH = 256
MLSTM_HEADS = 4
MLSTM_HEAD_DIM = 64
MLSTM_WIDTH = 256
CONV_WIDTH = 4
D_FF = 2816
N_EXPERTS = 8
D_FF_EXPERT = 3584
EPS = 1e-6
NEG_INF = -1e30

LANE = 128
HEAD_PAD = 128
ATTN_TILE = 256
SEQ_CHUNK = 256
SEQ_TAIL = 128
SEQ_BATCH = 2
FLAT_TILE = 512
FFN_ROW_TILE = 768
MOE_TILE = 1024
FF_TILE_DENSE = 1408
FF_TILE_EXPERT = 896
VMEM_LIMIT = 56 * 1024 * 1024

SEG_CQ = (0, 768)
SEG_CKV = (768, 1024)
SEG_XP = (1024, 1280)
SEG_MQK = (1280, 1792)
SEG_MV = (1792, 2048)
SEG_MO = (2048, 2304)
SEG_SMALL = (2304, 2432)
W_BIG = 2432


def _rms(x, g):
    ms = jnp.mean(x * x, axis=-1, keepdims=True)
    return x * lax.rsqrt(ms + EPS) * g


def _dot(a, b):
    return jnp.dot(a, b, preferred_element_type=F32)


def _dot_nt(a, b):
    return lax.dot_general(a, b, (((1,), (1,)), ((), ())), preferred_element_type=F32)


def _dot_tn(a, b):
    return lax.dot_general(a, b, (((0,), (0,)), ((), ())), preferred_element_type=F32)


def _dot_f32(a, b):
    return jnp.dot(a, b, preferred_element_type=F32, precision=lax.Precision.HIGHEST)


def _log_sigmoid(x):
    return jnp.minimum(x, 0.0) - jnp.log(1.0 + jnp.exp(-jnp.abs(x)))


def _pick_tile(n, target):
    best = None
    for t in range(16, target + 1, 16):
        if n % t == 0:
            best = t
    assert best is not None, (n, target)
    return best


def _cparams(sem):
    return pltpu.CompilerParams(dimension_semantics=sem, vmem_limit_bytes=VMEM_LIMIT)


def _rope_lanes(x, c, s_lo, s_hi):
    return x * c + pltpu.roll(x, LANE - 16, 1) * s_lo + pltpu.roll(x, 16, 1) * s_hi


def _inproj_kernel(h_ref, an_ref, wbig_ref, qn_ref, wq_ref, kvn_ref, wk_ref, wv_ref, tab_ref,
                   q_ref, k_ref, v_ref, xp_ref, mqk_ref, mv_ref, mo_ref, zs_ref):
    xn = _rms(h_ref[0], an_ref[...]).astype(BF16)

    def seg(ab):
        return _dot(xn, wbig_ref[:, ab[0]:ab[1]])

    xp_ref[0] = seg(SEG_XP).astype(BF16)
    mqk_ref[0] = seg(SEG_MQK).astype(BF16)
    mv_ref[0] = seg(SEG_MV).astype(BF16)
    mo_ref[0] = seg(SEG_MO).astype(BF16)
    zs = seg(SEG_SMALL)
    zs_ref[0] = zs

    tab = tab_ref[...]
    cq, s1q, s2q = tab[:, 0:128], tab[:, 128:256], tab[:, 256:384]
    ck, s1k, s2k = tab[:, 384:512], tab[:, 512:640], tab[:, 640:768]

    cqn = _rms(seg(SEG_CQ), qn_ref[...]).astype(BF16)
    qf = _dot(cqn, wq_ref[...])
    for hd in range(MLA_HEADS):
        sl = slice(HEAD_PAD * hd, HEAD_PAD * (hd + 1))
        q_ref[0, hd] = _rope_lanes(qf[:, sl], cq, s1q, s2q).astype(BF16)

    ckvn = _rms(seg(SEG_CKV), kvn_ref[...]).astype(BF16)
    v_ref[0] = _dot(ckvn, wv_ref[...]).astype(BF16)
    kf = _dot(ckvn, wk_ref[...])
    kr = pltpu.roll(_rope_lanes(zs, ck, s1k, s2k), QK_NOPE - 2 * MLSTM_HEADS, 1)
    for hd in range(MLA_HEADS):
        sl = slice(HEAD_PAD * hd, HEAD_PAD * (hd + 1))
        k_ref[0, hd] = (kf[:, sl] + kr).astype(BF16)


def _inproj(h, an, wbig, qn, wq, kvn, wk, wv, tab):
    B, L, D = h.shape
    T = _pick_tile(L, 768)
    nt = L // T
    row = lambda c: pl.BlockSpec((1, T, c), lambda b, i: (b, i, 0))
    full = lambda a: pl.BlockSpec(a.shape, lambda b, i: (0,) * a.ndim)
    widths = (MLA_HEADS * V_HEAD, POOL_WIDTH, 2 * MLSTM_WIDTH, MLSTM_WIDTH, MLSTM_WIDTH)
    head_major = jax.ShapeDtypeStruct((B, MLA_HEADS, L, HEAD_PAD), BF16)
    head_spec = pl.BlockSpec((1, MLA_HEADS, T, HEAD_PAD), lambda b, i: (b, 0, i, 0))
    out_shape = [head_major, head_major] + [jax.ShapeDtypeStruct((B, L, c), BF16) for c in widths]
    out_shape.append(jax.ShapeDtypeStruct((B, L, LANE), F32))
    out_specs = [head_spec, head_spec] + [row(c) for c in widths] + [row(LANE)]
    return pl.pallas_call(
        _inproj_kernel,
        grid=(B, nt),
        in_specs=[row(D), full(an), full(wbig), full(qn), full(wq), full(kvn), full(wk), full(wv),
                  pl.BlockSpec((T, tab.shape[1]), lambda b, i: (i, 0))],
        out_specs=out_specs,
        out_shape=out_shape,
        compiler_params=_cparams(("parallel", "parallel")),
        name="inproj",
    )(h, an, wbig, qn, wq, kvn, wk, wv, tab)


def _attn_kernel(q_ref, k_ref, v_ref, o_ref, vm_ref, s_ref, p_ref, oacc_ref, *, n_tiles):
    A = ATTN_TILE
    lane = lax.broadcasted_iota(jnp.int32, (1, LANE), 1)
    ri = lax.broadcasted_iota(jnp.int32, (A, A), 0) // CHUNK
    ci = lax.broadcasted_iota(jnp.int32, (A, A), 1) // CHUNK
    diag_mask = ci <= ri
    oacc_ref[...] = jnp.zeros_like(oacc_ref)

    def head_body(hh, carry):
        own = lane // V_HEAD == hh
        den = lane == V_HEAD * (1 - hh)
        vm_ref[...] = jnp.where(own, v_ref[0], jnp.where(den, 1.0, 0.0).astype(BF16))

        def finish(acc):
            l = jnp.sum(jnp.where(den, acc, 0.0), axis=-1, keepdims=True)
            return jnp.where(own, acc / l, 0.0)

        s0 = _dot_nt(q_ref[0, hh, 0:N_META, :], k_ref[0, hh, 0:N_META, :])
        p0 = jnp.exp2(s0 - jnp.max(s0, axis=-1, keepdims=True))
        oacc_ref[0:N_META, :] += finish(_dot(p0.astype(BF16), vm_ref[0:N_META, :]))

        for j in range(n_tiles):
            r0 = N_META + A * j
            qj = q_ref[0, hh, r0:r0 + A, :]
            s_meta = _dot_nt(qj, k_ref[0, hh, 0:N_META, :])
            m128 = None
            for c in range(j + 1):
                s = _dot_nt(qj, k_ref[0, hh, N_META + A * c:N_META + A * (c + 1), :])
                if c == j:
                    s = jnp.where(diag_mask, s, NEG_INF)
                s_ref[:, A * c:A * (c + 1)] = s
                for half in range(A // LANE):
                    lo = A * c + LANE * half
                    mc = s_ref[:, lo:lo + LANE]
                    m128 = mc if m128 is None else jnp.maximum(m128, mc)
            m = jnp.maximum(jnp.max(m128, axis=-1, keepdims=True),
                            jnp.max(s_meta, axis=-1, keepdims=True))
            for lo in range(0, A * (j + 1), LANE):
                p_ref[:, lo:lo + LANE] = jnp.exp2(s_ref[:, lo:lo + LANE] - m).astype(BF16)
            n = A * (j + 1)
            acc = (_dot(p_ref[:, 0:n], vm_ref[N_META:N_META + n, :])
                   + _dot(jnp.exp2(s_meta - m).astype(BF16), vm_ref[0:N_META, :]))
            oacc_ref[r0:r0 + A, :] += finish(acc)
        return carry

    lax.fori_loop(0, 2, head_body, 0)
    o_ref[0] = oacc_ref[...].astype(o_ref.dtype)


def _attention(q, k, v):
    B, _, L, _ = q.shape
    n_tiles = (L - N_META) // ATTN_TILE
    assert N_META + n_tiles * ATTN_TILE == L
    n_pairs = MLA_HEADS // 2
    qk_spec = pl.BlockSpec((1, 2, L, HEAD_PAD), lambda b, p: (b, p, 0, 0))
    v_spec = pl.BlockSpec((1, L, 2 * V_HEAD), lambda b, p: (b, 0, p))
    return pl.pallas_call(
        functools.partial(_attn_kernel, n_tiles=n_tiles),
        grid=(B, n_pairs),
        in_specs=[qk_spec, qk_spec, v_spec],
        out_specs=v_spec,
        out_shape=jax.ShapeDtypeStruct((B, L, MLA_HEADS * V_HEAD), BF16),
        scratch_shapes=[pltpu.VMEM((L, 2 * V_HEAD), BF16),
                        pltpu.VMEM((ATTN_TILE, n_tiles * ATTN_TILE), F32),
                        pltpu.VMEM((ATTN_TILE, n_tiles * ATTN_TILE), BF16),
                        pltpu.VMEM((L, 2 * V_HEAD), F32)],
        compiler_params=_cparams(("parallel", "parallel")),
        name="attention",
    )(q, k, v)


def _seq_kernel(xp_ref, mqk_ref, mv_ref, mo_ref, zs_ref, pw_ref, ps_ref,
                cw_ref, cb_ref, gbc_ref, mn_ref, yp_ref, ym_ref, ct_ref, m_ref, *, n_full, n_batch):
    HIST = 16
    L = xp_ref.shape[1]
    lane = lax.broadcasted_iota(jnp.int32, (1, LANE), 1)
    lane_lo = lane < MLSTM_HEAD_DIM
    lane256 = lax.broadcasted_iota(jnp.int32, (1, POOL_WIDTH), 1)
    grp = lane256 // POOL_GROUP_DIM
    win = jnp.left_shift(2, grp).astype(F32)
    row8 = lax.broadcasted_iota(jnp.int32, (2 * MLSTM_HEADS, 1), 0)

    ct_ref[...] = jnp.zeros_like(ct_ref)
    m_ref[...] = jnp.zeros_like(m_ref)

    def chunk(T, bb, gcol_raw, xx_p, xx_c, mv_c, mo_c, pos0, store, update_state=True):
        ti = lax.broadcasted_iota(jnp.int32, (T, T), 0)
        si = lax.broadcasted_iota(jnp.int32, (T, T), 1)
        causal = si <= ti
        tril = causal.astype(F32)
        triu = (ti <= si).astype(F32)
        t_col = lax.broadcasted_iota(jnp.int32, (T, 1), 0)
        s2 = xx_p + pltpu.roll(xx_p, 1, 0)
        s4 = s2 + pltpu.roll(s2, 2, 0)
        s8 = s4 + pltpu.roll(s4, 4, 0)
        s16 = s8 + pltpu.roll(s8, 8, 0)
        ssum = jnp.where(grp == 0, s2[HIST:], jnp.where(grp == 1, s4[HIST:],
                         jnp.where(grp == 2, s8[HIST:], s16[HIST:])))
        cnt = jnp.maximum(pos0 + t_col + 1, 1).astype(F32)
        d = ssum / jnp.minimum(cnt, win) - xx_p[HIST:]
        yp = _dot(d.astype(BF16), pw_ref[...]) * ps_ref[...]

        cw = cw_ref[...]
        y = (xx_c * cw[3:4] + pltpu.roll(xx_c, 1, 0) * cw[2:3] + pltpu.roll(xx_c, 2, 0) * cw[1:2]
             + pltpu.roll(xx_c, 3, 0) * cw[0:1])[HIST:] + cb_ref[...]
        qk = y * jax.nn.sigmoid(y)
        qc = qk[:, :MLSTM_WIDTH]
        kc = qk[:, MLSTM_WIDTH:] * (MLSTM_HEAD_DIM ** -0.5)

        gcol = gcol_raw + gbc_ref[...]
        grow = gcol.T[0:2 * MLSTM_HEADS, :]
        is_f = jnp.logical_and(lane >= MLSTM_HEADS, lane < 2 * MLSTM_HEADS)
        b_col = _dot_f32(tril, jnp.where(is_f, _log_sigmoid(gcol), 0.0))
        b_row = _dot_f32(jnp.where(row8 >= MLSTM_HEADS, _log_sigmoid(grow), 0.0), triu)

        ym = []
        for pr in range(MLSTM_HEADS // 2):
            psl = slice(LANE * pr, LANE * (pr + 1))
            q_pair, k_pair = qc[:, psl], kc[:, psl]
            v_pair = mv_c[:, psl]
            h_heads = []
            for e in range(2):
                hd = 2 * pr + e
                own = lane_lo if e == 0 else jnp.logical_not(lane_lo)
                den_lane = MLSTM_HEAD_DIM if e == 0 else 0
                q_h = jnp.where(own, q_pair, 0.0).astype(BF16)
                k_h = jnp.where(own, k_pair, 0.0).astype(BF16)
                v_aug = jnp.where(own, v_pair, jnp.where(lane == den_lane, 1.0, 0.0))
                bc = b_col[:, MLSTM_HEADS + hd:MLSTM_HEADS + hd + 1]
                lic = gcol[:, hd:hd + 1]
                br = b_row[MLSTM_HEADS + hd:MLSTM_HEADS + hd + 1, :]
                lir = grow[hd:hd + 1, :]
                dm = jnp.where(causal, bc + (lir - br), NEG_INF)
                m_prev = m_ref[bb, hd][:, 0:1]
                m_inter = bc + m_prev
                mt = jnp.maximum(m_inter, jnp.max(dm, axis=-1, keepdims=True))
                s = _dot_nt(q_h, k_h) * jnp.exp(dm - mt)
                decay = jnp.exp(m_inter - mt)
                num = (_dot(s.astype(BF16), v_aug.astype(BF16))
                       + decay * _dot(q_h, ct_ref[bb, hd].astype(BF16)))
                den = num[:, den_lane:den_lane + 1]
                h_heads.append(num / jnp.maximum(jnp.abs(den), jnp.exp(-mt)))
                if not update_state:
                    continue
                b_last = bc[T - 1:T, :]
                g_col = b_last - bc + lic
                m_new = jnp.maximum(b_last + m_prev, jnp.max(g_col, axis=0, keepdims=True))
                wg = jnp.exp(g_col - m_new)
                cd = jnp.exp(b_last + m_prev - m_new)
                ct_ref[bb, hd] = cd * ct_ref[bb, hd] + _dot_tn(k_h, (v_aug * wg).astype(BF16))
                m_ref[bb, hd] = jnp.broadcast_to(m_new, (1, LANE))
            h_pair = jnp.where(lane_lo, h_heads[0], h_heads[1])
            sq = h_pair * h_pair
            ms = jnp.where(lane_lo,
                           jnp.sum(jnp.where(lane_lo, sq, 0.0), axis=-1, keepdims=True),
                           jnp.sum(jnp.where(lane_lo, 0.0, sq), axis=-1, keepdims=True)) / MLSTM_HEAD_DIM
            hn = h_pair * lax.rsqrt(ms + EPS) * mn_ref[:, psl]
            ym.append(hn * jax.nn.sigmoid(mo_c[:, psl]))
        store(yp, jnp.concatenate(ym, axis=-1))

    T = SEQ_CHUNK
    TT = SEQ_TAIL
    tail = L - n_full * T

    def store_rows(bb, rows, n=None):
        def store(yp, ym):
            yp_ref[bb, rows, :] = (yp if n is None else yp[:n]).astype(yp_ref.dtype)
            ym_ref[bb, rows, :] = (ym if n is None else ym[:n]).astype(ym_ref.dtype)
        return store

    def head(ref, bb, width):
        return jnp.concatenate([jnp.zeros((HIST, width), F32), ref[bb, 0:T, :].astype(F32)], axis=0)

    for bb in range(n_batch):
        chunk(T, bb, zs_ref[bb, 0:T, :], head(xp_ref, bb, POOL_WIDTH),
              head(mqk_ref, bb, 2 * MLSTM_WIDTH), mv_ref[bb, 0:T, :].astype(F32),
              mo_ref[bb, 0:T, :].astype(F32), 0, store_rows(bb, slice(0, T)))

    def body(c, carry):
        r0 = pl.multiple_of(T * c, T)
        ext = pl.ds(pl.multiple_of(r0 - HIST, 16), T + HIST)
        cur = pl.ds(r0, T)
        for bb in range(n_batch):
            chunk(T, bb, zs_ref[bb, cur, :], xp_ref[bb, ext, :].astype(F32),
                  mqk_ref[bb, ext, :].astype(F32), mv_ref[bb, cur, :].astype(F32),
                  mo_ref[bb, cur, :].astype(F32), r0, store_rows(bb, cur))
        return carry

    lax.fori_loop(1, n_full, body, 0)

    r0 = n_full * T

    def padded(ref, bb, width, lo):
        return jnp.concatenate([ref[bb, lo:L, :].astype(F32), jnp.zeros((TT - tail, width), F32)], axis=0)

    for bb in range(n_batch):
        closed = jnp.broadcast_to(jnp.where(lane < MLSTM_HEADS, NEG_INF, 1e4), (TT - tail, LANE))
        chunk(TT, bb, jnp.concatenate([zs_ref[bb, r0:L, :], closed], axis=0),
              padded(xp_ref, bb, POOL_WIDTH, r0 - HIST),
              padded(mqk_ref, bb, 2 * MLSTM_WIDTH, r0 - HIST), padded(mv_ref, bb, MLSTM_WIDTH, r0),
              padded(mo_ref, bb, MLSTM_WIDTH, r0), r0, store_rows(bb, slice(r0, L), tail),
              update_state=False)


def _seq_mixer(xp, mqk, mv, mo, zs, pw, ps, cw, cb, gbc, mn):
    B, L, _ = xp.shape
    n_full = L // SEQ_CHUNK
    assert 0 < L - n_full * SEQ_CHUNK <= SEQ_TAIL
    BB = SEQ_BATCH
    assert B % BB == 0
    seq = lambda c: pl.BlockSpec((BB, L, c), lambda b: (b, 0, 0))
    full = lambda a: pl.BlockSpec(a.shape, lambda b: (0,) * a.ndim)
    return pl.pallas_call(
        functools.partial(_seq_kernel, n_full=n_full, n_batch=BB),
        grid=(B // BB,),
        in_specs=[seq(POOL_WIDTH), seq(2 * MLSTM_WIDTH), seq(MLSTM_WIDTH), seq(MLSTM_WIDTH),
                  seq(LANE), full(pw), full(ps), full(cw), full(cb), full(gbc), full(mn)],
        out_specs=[seq(POOL_WIDTH), seq(MLSTM_WIDTH)],
        out_shape=[jax.ShapeDtypeStruct((B, L, POOL_WIDTH), BF16),
                   jax.ShapeDtypeStruct((B, L, MLSTM_WIDTH), BF16)],
        scratch_shapes=[pltpu.VMEM((BB, MLSTM_HEADS, LANE, LANE), F32),
                        pltpu.VMEM((BB, MLSTM_HEADS, 1, LANE), F32)],
        compiler_params=_cparams(("parallel",)),
        name="seq_mixer",
    )(xp, mqk, mv, mo, zs, pw, ps, cw, cb, gbc, mn)


def _ffn_kernel(h_ref, ya_ref, yp_ref, ym_ref, wa_ref, wp_ref, wm_ref, g_ref, wg_ref, wu_ref, wd_ref,
                o_ref, xn_ref, hid_ref):
    hf = (h_ref[...] + _dot(ya_ref[...], wa_ref[...]) + _dot(yp_ref[...], wp_ref[...])
          + _dot(ym_ref[...], wm_ref[...]))
    o_ref[...] = hf
    xn_ref[...] = _rms(hf, g_ref[...]).astype(BF16)
    ff = hid_ref.shape[1]
    for c0 in range(0, ff, FF_TILE_DENSE):
        c1 = c0 + FF_TILE_DENSE
        a = _dot(xn_ref[...], wg_ref[:, c0:c1])
        u = _dot(xn_ref[...], wu_ref[:, c0:c1])
        hid_ref[:, c0:c1] = (a * jax.nn.sigmoid(a) * u).astype(BF16)
    o_ref[...] += _dot(hid_ref[...], wd_ref[...])


def _ffn(h, ya, yp, ym, wa, wp, wm, g, wg, wu, wd):
    N, D = h.shape
    FF = wg.shape[1]
    T = _pick_tile(N, FFN_ROW_TILE)
    assert FF % FF_TILE_DENSE == 0
    row = lambda a: pl.BlockSpec((T, a.shape[1]), lambda i: (i, 0))
    full = lambda a: pl.BlockSpec(a.shape, lambda i: (0, 0), pipeline_mode=pl.Buffered(1))
    return pl.pallas_call(
        _ffn_kernel,
        grid=(N // T,),
        in_specs=[row(h), row(ya), row(yp), row(ym), full(wa), full(wp), full(wm), full(g),
                  full(wg), full(wu), full(wd)],
        out_specs=row(h),
        out_shape=jax.ShapeDtypeStruct((N, D), F32),
        scratch_shapes=[pltpu.VMEM((T, D), BF16), pltpu.VMEM((T, FF), BF16)],
        compiler_params=_cparams(("parallel",)),
        name="dense_ffn",
    )(h, ya, yp, ym, wa, wp, wm, g, wg, wu, wd)


def _router_kernel(h_ref, ya_ref, yp_ref, ym_ref, wa_ref, wp_ref, wm_ref, g_ref, wrh_ref, wrl_ref, tri_ref,
                   hf_ref, route_ref, cnt_ref, run_ref):
    i = pl.program_id(0)
    T = h_ref.shape[0]

    @pl.when(i == 0)
    def _():
        run_ref[...] = jnp.zeros_like(run_ref)

    hf = (h_ref[...] + _dot(ya_ref[...], wa_ref[...]) + _dot(yp_ref[...], wp_ref[...])
          + _dot(ym_ref[...], wm_ref[...]))
    hf_ref[...] = hf
    xn = _rms(hf, g_ref[...])
    lane = lax.broadcasted_iota(jnp.int32, (T, LANE), 1).astype(F32)
    x_hi = xn.astype(BF16)
    x_lo = (xn - x_hi.astype(F32)).astype(BF16)
    logits = _dot(x_hi, wrh_ref[...]) + (_dot(x_lo, wrh_ref[...]) + _dot(x_hi, wrl_ref[...]))
    logits = jnp.where(lane < N_EXPERTS, logits, NEG_INF)
    v1 = jnp.max(logits, axis=-1, keepdims=True)
    i1 = jnp.min(jnp.where(logits == v1, lane, float(LANE)), axis=-1, keepdims=True)
    rest = jnp.where(lane == i1, NEG_INF, logits)
    v2 = jnp.max(rest, axis=-1, keepdims=True)
    i2 = jnp.min(jnp.where(rest == v2, lane, float(LANE)), axis=-1, keepdims=True)
    e2 = jnp.exp(v2 - v1)
    g1 = 1.0 / (1.0 + e2)
    g2 = e2 / (1.0 + e2)
    member = jnp.logical_or(lane == i1, lane == i2).astype(F32)
    rank = run_ref[0:1, :] + _dot(tri_ref[...], member.astype(BF16))
    r1 = jnp.sum(jnp.where(lane == i1, rank, 0.0), axis=-1, keepdims=True)
    r2 = jnp.sum(jnp.where(lane == i2, rank, 0.0), axis=-1, keepdims=True)
    run_ref[...] = run_ref[...] + jnp.sum(member, axis=0, keepdims=True)
    cnt_ref[...] = run_ref[...]
    route = jnp.where(lane == 0, i1, jnp.where(lane == 1, i2,
            jnp.where(lane == 2, r1, jnp.where(lane == 3, r2,
            jnp.where(lane == 4, g1, jnp.where(lane == 5, g2, 0.0))))))
    route_ref[...] = route


def _router(h, ya, yp, ym, wa, wp, wm, g, wr):
    N, D = h.shape
    T = _pick_tile(N, FLAT_TILE)
    tri = jnp.tril(jnp.ones((T, T), BF16), -1)
    wr_hi = wr.astype(BF16)
    wr_lo = (wr - wr_hi.astype(F32)).astype(BF16)
    row = lambda a: pl.BlockSpec((T, a.shape[1]), lambda i: (i, 0))
    full = lambda a: pl.BlockSpec(a.shape, lambda i: (0, 0))
    return pl.pallas_call(
        _router_kernel,
        grid=(N // T,),
        in_specs=[row(h), row(ya), row(yp), row(ym), full(wa), full(wp), full(wm), full(g), full(wr_hi),
                  full(wr_lo), full(tri)],
        out_specs=[pl.BlockSpec((T, D), lambda i: (i, 0)),
                   pl.BlockSpec((T, LANE), lambda i: (i, 0)),
                   pl.BlockSpec((8, LANE), lambda i: (0, 0))],
        out_shape=[jax.ShapeDtypeStruct((N, D), F32), jax.ShapeDtypeStruct((N, LANE), F32),
                   jax.ShapeDtypeStruct((8, LANE), F32)],
        scratch_shapes=[pltpu.VMEM((8, LANE), F32)],
        compiler_params=_cparams(("arbitrary",)),
        name="router",
    )(h, ya, yp, ym, wa, wp, wm, g, wr_hi, wr_lo, tri)


def _dispatch_kernel(dest_ref, h_ref, xs_in_ref, xs_ref, sem):
    del xs_in_ref
    n = h_ref.shape[0]

    def copy(t, slot):
        return pltpu.make_async_copy(h_ref.at[pl.ds(t, 1)],
                                     xs_ref.at[pl.ds(dest_ref[0, 0, 2 * t + slot], 1)], sem)

    def issue(t, c):
        copy(t, 0).start()
        copy(t, 1).start()
        return c

    lax.fori_loop(0, n, issue, 0, unroll=8)

    def drain(t, c):
        copy(t, 0).wait()
        copy(t, 1).wait()
        return c

    lax.fori_loop(0, n, drain, 0, unroll=8)


def _dispatch(dest3, h, xs_init):
    n_tiles = dest3.shape[0]
    return pl.pallas_call(
        _dispatch_kernel,
        grid=(n_tiles,),
        in_specs=[pl.BlockSpec((1, 1, dest3.shape[2]), lambda i: (i, 0, 0), memory_space=pltpu.SMEM),
                  pl.BlockSpec((dest3.shape[2] // 2, h.shape[1]), lambda i: (i, 0)),
                  pl.BlockSpec(memory_space=pl.ANY)],
        out_specs=pl.BlockSpec(memory_space=pl.ANY),
        out_shape=jax.ShapeDtypeStruct(xs_init.shape, xs_init.dtype),
        scratch_shapes=[pltpu.SemaphoreType.DMA(())],
        input_output_aliases={2: 0},
        compiler_params=_cparams(("arbitrary",)),
        name="dispatch",
    )(dest3, h, xs_init)


def _expert_kernel(be_ref, x_ref, g_ref, wg_ref, wu_ref, wd_ref, o_ref, xn_ref):
    r = pl.program_id(0)
    j = pl.program_id(1)
    n_active = be_ref[be_ref.shape[0] - 1]

    @pl.when(r < n_active)
    def _():
        @pl.when(j == 0)
        def _():
            xn_ref[...] = _rms(x_ref[...], g_ref[...]).astype(BF16)

        xn = xn_ref[...]
        a = _dot(xn, wg_ref[0])
        u = _dot(xn, wu_ref[0])
        y = _dot((a * jax.nn.sigmoid(a) * u).astype(BF16), wd_ref[0])

        @pl.when(j == 0)
        def _():
            o_ref[...] = y

        @pl.when(j > 0)
        def _():
            o_ref[...] += y

    @pl.when(jnp.logical_and(r >= n_active, j == 0))
    def _():
        o_ref[...] = jnp.zeros_like(o_ref)


def _experts(block_exp, xs, g, wg, wu, wd):
    R, D = xs.shape
    TM = MOE_TILE
    FF = wg.shape[2]
    TF = FF_TILE_EXPERT
    assert FF % TF == 0 and R % TM == 0
    grid_spec = pltpu.PrefetchScalarGridSpec(
        num_scalar_prefetch=1,
        grid=(R // TM, FF // TF),
        in_specs=[pl.BlockSpec((TM, D), lambda r, j, be: (r, 0)),
                  pl.BlockSpec(g.shape, lambda r, j, be: (0, 0)),
                  pl.BlockSpec((1, D, TF), lambda r, j, be: (be[r], 0, j)),
                  pl.BlockSpec((1, D, TF), lambda r, j, be: (be[r], 0, j)),
                  pl.BlockSpec((1, TF, D), lambda r, j, be: (be[r], j, 0))],
        out_specs=pl.BlockSpec((TM, D), lambda r, j, be: (r, 0)),
        scratch_shapes=[pltpu.VMEM((TM, D), BF16)],
    )
    return pl.pallas_call(
        _expert_kernel,
        grid_spec=grid_spec,
        out_shape=jax.ShapeDtypeStruct((R, D), F32),
        compiler_params=_cparams(("parallel", "arbitrary")),
        name="experts",
    )(block_exp, xs, g, wg, wu, wd)


def _combine_kernel(dest_ref, h_ref, route_ref, ys_ref, o_ref, buf_ref, sem):
    T = h_ref.shape[0]

    def copy(t, slot):
        return pltpu.make_async_copy(ys_ref.at[pl.ds(dest_ref[0, 0, 2 * t + slot], 1)],
                                     buf_ref.at[slot, pl.ds(t, 1)], sem)

    def issue(t, c):
        copy(t, 0).start()
        copy(t, 1).start()
        return c

    lax.fori_loop(0, T, issue, 0, unroll=8)

    def drain(t, c):
        copy(t, 0).wait()
        copy(t, 1).wait()
        return c

    lax.fori_loop(0, T, drain, 0, unroll=8)
    route = route_ref[...]
    o_ref[...] = h_ref[...] + route[:, 4:5] * buf_ref[0] + route[:, 5:6] * buf_ref[1]


def _combine(dest3, h, route, ys):
    N, D = h.shape
    T = dest3.shape[2] // 2
    return pl.pallas_call(
        _combine_kernel,
        grid=(N // T,),
        in_specs=[pl.BlockSpec((1, 1, 2 * T), lambda i: (i, 0, 0), memory_space=pltpu.SMEM),
                  pl.BlockSpec((T, D), lambda i: (i, 0)),
                  pl.BlockSpec((T, LANE), lambda i: (i, 0)),
                  pl.BlockSpec(memory_space=pl.ANY)],
        out_specs=pl.BlockSpec((T, D), lambda i: (i, 0)),
        out_shape=jax.ShapeDtypeStruct((N, D), F32),
        scratch_shapes=[pltpu.VMEM((2, T, D), F32), pltpu.SemaphoreType.DMA(())],
        compiler_params=_cparams(("arbitrary",)),
        name="combine",
    )(dest3, h, route, ys)


def _moe(mixed, g, wr, wg, wu, wd):
    N, D = mixed[0].shape
    T = _pick_tile(N, FLAT_TILE)
    h, route, counts = _router(*mixed, g, wr)
    cnt = counts[0, :N_EXPERTS].astype(jnp.int32)
    padded = (cnt + MOE_TILE - 1) // MOE_TILE * MOE_TILE
    pend = jnp.cumsum(padded)
    pstart = pend - padded
    idx = route[:, 0:2].astype(jnp.int32)
    dest = pstart[idx] + route[:, 2:4].astype(jnp.int32)
    dest3 = dest.reshape(N // T, 1, 2 * T)
    n_blocks = -(-2 * N // MOE_TILE) + N_EXPERTS
    block_exp = jnp.minimum(jnp.searchsorted(pend, jnp.arange(n_blocks) * MOE_TILE, side='right'),
                            N_EXPERTS - 1).astype(jnp.int32)
    block_exp = jnp.concatenate([block_exp, (pend[-1:] // MOE_TILE).astype(jnp.int32)])
    xs = _dispatch(dest3, h, jnp.zeros((n_blocks * MOE_TILE, D), F32))
    ys = _experts(block_exp, xs, g, wg, wu, wd)
    return _combine(dest3, h, route, ys)


def _final_kernel(h_ref, g_ref, o_ref):
    o_ref[0] = _rms(h_ref[0, N_META:, :], g_ref[...])


def _final(h, g):
    B, L, D = h.shape
    return pl.pallas_call(
        _final_kernel,
        grid=(B,),
        in_specs=[pl.BlockSpec((1, L, D), lambda b: (b, 0, 0)), pl.BlockSpec(g.shape, lambda b: (0, 0))],
        out_specs=pl.BlockSpec((1, L - N_META, D), lambda b: (b, 0, 0)),
        out_shape=jax.ShapeDtypeStruct((B, L - N_META, D), F32),
        compiler_params=_cparams(("parallel",)),
        name="final_norm",
    )(h, g)


def _rope_tables(L):
    pos = jnp.arange(L, dtype=jnp.int32)
    inv_freq = ROPE_THETA ** (-jnp.arange(0, QK_ROPE, 2, dtype=F32) / QK_ROPE)
    ang = pos.astype(F32)[:, None] * inv_freq[None, :]
    cos, sin = jnp.cos(ang), jnp.sin(ang)
    z = lambda n: jnp.zeros((L, n), F32)
    scale = (QK_NOPE + QK_ROPE) ** -0.5 * float(np.log2(np.e))
    cq = jnp.concatenate([jnp.ones((L, QK_NOPE), F32), cos, cos, z(32)], axis=1) * scale
    s1q = jnp.concatenate([z(QK_NOPE), -sin, z(48)], axis=1) * scale
    s2q = jnp.concatenate([z(QK_NOPE + 16), sin, z(32)], axis=1) * scale
    g = 2 * MLSTM_HEADS
    ck = jnp.concatenate([z(g), cos, cos, z(96 - g)], axis=1)
    s1k = jnp.concatenate([z(g), -sin, z(112 - g)], axis=1)
    s2k = jnp.concatenate([z(g + 16), sin, z(96 - g)], axis=1)
    return jnp.concatenate([cq, s1q, s2q, ck, s1k, s2k], axis=1)


def _layer_weights(l, w_in, w_q_up, w_kv_up, pool_w, w_out):
    pts = np.cumsum([Q_LORA, KV_LORA, QK_ROPE, POOL_WIDTH, 2 * MLSTM_WIDTH, MLSTM_WIDTH, MLSTM_WIDTH]).tolist()
    wi = w_in[l]
    w_cq, w_ckv, w_kr, w_xp, w_mqk, w_mv, w_mo, w_mg = jnp.split(wi, pts, axis=1)
    small = jnp.concatenate([w_mg, w_kr, jnp.zeros((D_MODEL, LANE - QK_ROPE - 2 * MLSTM_HEADS), F32)], axis=1)
    wbig = jnp.concatenate([w_cq, w_ckv, w_xp, w_mqk, w_mv, w_mo, small], axis=1).astype(BF16)
    wq = w_q_up[l].reshape(Q_LORA, MLA_HEADS, QK_NOPE + QK_ROPE)
    wq = jnp.pad(wq, ((0, 0), (0, 0), (0, HEAD_PAD - QK_NOPE - QK_ROPE))).reshape(Q_LORA, -1).astype(BF16)
    wkv = w_kv_up[l].reshape(KV_LORA, MLA_HEADS, QK_NOPE + V_HEAD)
    wk = jnp.pad(wkv[:, :, :QK_NOPE], ((0, 0), (0, 0), (0, HEAD_PAD - QK_NOPE))).reshape(KV_LORA, -1).astype(BF16)
    wv = wkv[:, :, QK_NOPE:].reshape(KV_LORA, -1).astype(BF16)
    pw = jax.scipy.linalg.block_diag(*[pool_w[l, g] for g in range(POOL_GROUPS)]).astype(BF16)
    wo = w_out[l].astype(BF16)
    n_attn = MLA_HEADS * V_HEAD
    return dict(wbig=wbig, wq=wq, wk=wk, wv=wv, pw=pw,
                wa=wo[:n_attn], wp=wo[n_attn:n_attn + POOL_WIDTH], wm=wo[n_attn + POOL_WIDTH:])


def kernel(x, meta_tokens, attn_norm, w_in, q_norm, w_q_up, kv_norm, w_kv_up, pool_w, pool_scale, conv_w, conv_b, gate_bias, mlstm_norm, w_out, ffn_norm, dense_w_gate, dense_w_up, dense_w_down, router_w, moe_w_gate, moe_w_up, moe_w_down, final_norm):
    B, S, D = x.shape
    L = N_META + S
    N = B * L
    depth = w_in.shape[0]
    n_full = L // SEQ_CHUNK
    assert S % ATTN_TILE == 0 and S % SEQ_CHUNK == 0

    meta = jnp.broadcast_to(meta_tokens[None].astype(x.dtype), (B, N_META, D))
    h = jnp.concatenate([meta, x], axis=1)
    tab = _rope_tables(L)
    row = lambda a: a.reshape(1, -1).astype(F32)

    for l in range(depth):
        lw = _layer_weights(l, w_in, w_q_up, w_kv_up, pool_w, w_out)
        q, k, v, xp, mqk, mv, mo, zs = _inproj(h, row(attn_norm[l]), lw['wbig'], row(q_norm[l]), lw['wq'],
                                               row(kv_norm[l]), lw['wk'], lw['wv'], tab)
        ya = _attention(q, k, v)
        gbc = jnp.pad(gate_bias[l], (0, LANE - 2 * MLSTM_HEADS)).reshape(1, LANE).astype(F32)
        yp, ym = _seq_mixer(xp, mqk, mv, mo, zs, lw['pw'], row(pool_scale[l]),
                            conv_w[l].astype(F32), row(conv_b[l]), gbc, row(mlstm_norm[l]))
        mixed = (h.reshape(N, D), ya.reshape(N, -1), yp.reshape(N, -1), ym.reshape(N, -1),
                 lw['wa'], lw['wp'], lw['wm'])
        j = l // 2
        if l % 2 == 0:
            hf = _ffn(*mixed, row(ffn_norm[l]), dense_w_gate[j].astype(BF16), dense_w_up[j].astype(BF16),
                      dense_w_down[j].astype(BF16))
        else:
            wr = jnp.pad(router_w[j], ((0, 0), (0, LANE - N_EXPERTS))).astype(F32)
            hf = _moe(mixed, row(ffn_norm[l]), wr, moe_w_gate[j].astype(BF16), moe_w_up[j].astype(BF16),
                      moe_w_down[j].astype(BF16))
        h = hf.reshape(B, L, D)
    return _final(h, row(final_norm))
```

```python
import functools

import numpy as np
import jax
import jax.numpy as jnp
from jax import lax
from jax.experimental import pallas as pl
from jax.experimental.pallas import tpu as pltpu

F32 = jnp.float32
BF16 = jnp.bfloat16

D_MODEL = 1024
N_META = 16
CHUNK = 64
MLA_HEADS = 8
QK_NOPE = 64
QK_ROPE = 32
V_HEAD = 64
Q_LORA = 768
KV_LORA = 256
ROPE_THETA = 10000.0
POOL_GROUPS = 4
POOL_GROUP_DIM = 64
POOL_WIDTH = 256
MLSTM_HEADS = 4
MLSTM_HEAD_DIM = 64
MLSTM_WIDTH = 256
CONV_WIDTH = 4
D_FF = 2816
N_EXPERTS = 8
D_FF_EXPERT = 3584
EPS = 1e-6
NEG_INF = -1e30

LANE = 128
HEAD_PAD = 128
ATTN_TILE = 256
SEQ_CHUNK = 256
SEQ_TAIL = 128
SEQ_BATCH = 2
FLAT_TILE = 512
FFN_ROW_TILE = 768
MOE_TILE = 1024
FF_TILE_DENSE = 1408
FF_TILE_EXPERT = 896
VMEM_LIMIT = 56 * 1024 * 1024

SEG_CQ = (0, 768)
SEG_CKV = (768, 1024)
SEG_XP = (1024, 1280)
SEG_MQK = (1280, 1792)
SEG_MV = (1792, 2048)
SEG_MO = (2048, 2304)
SEG_SMALL = (2304, 2432)
W_BIG = 2432


def _rms(x, g):
    ms = jnp.mean(x * x, axis=-1, keepdims=True)
    return x * lax.rsqrt(ms + EPS) * g


def _dot(a, b):
    return jnp.dot(a, b, preferred_element_type=F32)


def _dot_nt(a, b):
    return lax.dot_general(a, b, (((1,), (1,)), ((), ())), preferred_element_type=F32)


def _dot_tn(a, b):
    return lax.dot_general(a, b, (((0,), (0,)), ((), ())), preferred_element_type=F32)


def _dot_f32(a, b):
    return jnp.dot(a, b, preferred_element_type=F32, precision=lax.Precision.HIGHEST)


def _log_sigmoid(x):
    return jnp.minimum(x, 0.0) - jnp.log(1.0 + jnp.exp(-jnp.abs(x)))


def _pick_tile(n, target):
    best = None
    for t in range(16, target + 1, 16):
        if n % t == 0:
            best = t
    assert best is not None, (n, target)
    return best


def _cparams(sem):
    return pltpu.CompilerParams(dimension_semantics=sem, vmem_limit_bytes=VMEM_LIMIT)


def _rope_lanes(x, c, s_lo, s_hi):
    return x * c + pltpu.roll(x, LANE - 16, 1) * s_lo + pltpu.roll(x, 16, 1) * s_hi


def _inproj_kernel(h_ref, an_ref, wbig_ref, qn_ref, wq_ref, kvn_ref, wk_ref, wv_ref, tab_ref,
                   q_ref, k_ref, v_ref, xp_ref, mqk_ref, mv_ref, mo_ref, zs_ref):
    xn = _rms(h_ref[0], an_ref[...]).astype(BF16)

    z = _dot(xn, wbig_ref[...])

    def seg(ab):
        return z[:, ab[0]:ab[1]]

    xp_ref[0] = seg(SEG_XP).astype(BF16)
    mqk_ref[0] = seg(SEG_MQK).astype(BF16)
    mv_ref[0] = seg(SEG_MV).astype(BF16)
    mo_ref[0] = seg(SEG_MO).astype(BF16)
    zs = seg(SEG_SMALL)
    zs_ref[0] = zs

    tab = tab_ref[...]
    cq, s1q, s2q = tab[:, 0:128], tab[:, 128:256], tab[:, 256:384]
    ck, s1k, s2k = tab[:, 384:512], tab[:, 512:640], tab[:, 640:768]

    cqn = _rms(seg(SEG_CQ), qn_ref[...]).astype(BF16)
    qf = _dot(cqn, wq_ref[...])
    for hd in range(MLA_HEADS):
        sl = slice(HEAD_PAD * hd, HEAD_PAD * (hd + 1))
        q_ref[0, hd] = _rope_lanes(qf[:, sl], cq, s1q, s2q).astype(BF16)

    ckvn = _rms(seg(SEG_CKV), kvn_ref[...]).astype(BF16)
    v_ref[0] = _dot(ckvn, wv_ref[...]).astype(BF16)
    kf = _dot(ckvn, wk_ref[...])
    kr = pltpu.roll(_rope_lanes(zs, ck, s1k, s2k), QK_NOPE - 2 * MLSTM_HEADS, 1)
    for hd in range(MLA_HEADS):
        sl = slice(HEAD_PAD * hd, HEAD_PAD * (hd + 1))
        k_ref[0, hd] = (kf[:, sl] + kr).astype(BF16)


def _inproj(h, an, wbig, qn, wq, kvn, wk, wv, tab):
    B, L, D = h.shape
    T = _pick_tile(L, 768)
    nt = L // T
    row = lambda c: pl.BlockSpec((1, T, c), lambda b, i: (b, i, 0))
    full = lambda a: pl.BlockSpec(a.shape, lambda b, i: (0,) * a.ndim)
    widths = (MLA_HEADS * V_HEAD, POOL_WIDTH, 2 * MLSTM_WIDTH, MLSTM_WIDTH, MLSTM_WIDTH)
    head_major = jax.ShapeDtypeStruct((B, MLA_HEADS, L, HEAD_PAD), BF16)
    head_spec = pl.BlockSpec((1, MLA_HEADS, T, HEAD_PAD), lambda b, i: (b, 0, i, 0))
    out_shape = [head_major, head_major] + [jax.ShapeDtypeStruct((B, L, c), BF16) for c in widths]
    out_shape.append(jax.ShapeDtypeStruct((B, L, LANE), F32))
    out_specs = [head_spec, head_spec] + [row(c) for c in widths] + [row(LANE)]
    return pl.pallas_call(
        _inproj_kernel,
        grid=(B, nt),
        in_specs=[row(D), full(an), full(wbig), full(qn), full(wq), full(kvn), full(wk), full(wv),
                  pl.BlockSpec((T, tab.shape[1]), lambda b, i: (i, 0))],
        out_specs=out_specs,
        out_shape=out_shape,
        compiler_params=_cparams(("parallel", "parallel")),
        name="inproj",
    )(h, an, wbig, qn, wq, kvn, wk, wv, tab)


def _attn_kernel(q_ref, k_ref, v_ref, o_ref, vm_ref, s_ref, p_ref, oacc_ref, *, n_tiles):
    A = ATTN_TILE
    lane = lax.broadcasted_iota(jnp.int32, (1, LANE), 1)
    ri = lax.broadcasted_iota(jnp.int32, (A, A), 0) // CHUNK
    ci = lax.broadcasted_iota(jnp.int32, (A, A), 1) // CHUNK
    diag_mask = ci <= ri
    oacc_ref[...] = jnp.zeros_like(oacc_ref)

    def head_body(hh, carry):
        vm_ref[...] = jnp.where(lane // V_HEAD == hh, v_ref[0], jnp.zeros_like(v_ref[0]))

        s0 = _dot_nt(q_ref[0, hh, 0:N_META, :], k_ref[0, hh, 0:N_META, :])
        p0 = jnp.exp2(s0 - jnp.max(s0, axis=-1, keepdims=True))
        o0 = _dot(p0.astype(BF16), vm_ref[0:N_META, :]) / jnp.sum(p0, axis=-1, keepdims=True)
        oacc_ref[0:N_META, :] += o0

        for j in range(n_tiles):
            r0 = N_META + A * j
            qj = q_ref[0, hh, r0:r0 + A, :]
            s_meta = _dot_nt(qj, k_ref[0, hh, 0:N_META, :])
            m128 = None
            for c in range(j + 1):
                s = _dot_nt(qj, k_ref[0, hh, N_META + A * c:N_META + A * (c + 1), :])
                if c == j:
                    s = jnp.where(diag_mask, s, NEG_INF)
                s_ref[:, A * c:A * (c + 1)] = s
                mc = jnp.maximum(s[:, :LANE], s[:, LANE:])
                m128 = mc if m128 is None else jnp.maximum(m128, mc)
            m = jnp.maximum(jnp.max(m128, axis=-1, keepdims=True),
                            jnp.max(s_meta, axis=-1, keepdims=True))
            p_meta = jnp.exp2(s_meta - m)
            l128 = jnp.zeros((A, LANE), F32)
            for c in range(j + 1):
                p = jnp.exp2(s_ref[:, A * c:A * (c + 1)] - m)
                l128 = l128 + p[:, :LANE] + p[:, LANE:]
                p_ref[:, A * c:A * (c + 1)] = p.astype(BF16)
            l = jnp.sum(p_meta, axis=-1, keepdims=True) + jnp.sum(l128, axis=-1, keepdims=True)
            n = A * (j + 1)
            acc = (_dot(p_ref[:, 0:n], vm_ref[N_META:N_META + n, :])
                   + _dot(p_meta.astype(BF16), vm_ref[0:N_META, :]))
            oacc_ref[r0:r0 + A, :] += acc / l
        return carry

    lax.fori_loop(0, 2, head_body, 0)
    o_ref[0] = oacc_ref[...].astype(o_ref.dtype)


def _attention(q, k, v):
    B, _, L, _ = q.shape
    n_tiles = (L - N_META) // ATTN_TILE
    assert N_META + n_tiles * ATTN_TILE == L
    n_pairs = MLA_HEADS // 2
    qk_spec = pl.BlockSpec((1, 2, L, HEAD_PAD), lambda b, p: (b, p, 0, 0))
    v_spec = pl.BlockSpec((1, L, 2 * V_HEAD), lambda b, p: (b, 0, p))
    return pl.pallas_call(
        functools.partial(_attn_kernel, n_tiles=n_tiles),
        grid=(B, n_pairs),
        in_specs=[qk_spec, qk_spec, v_spec],
        out_specs=v_spec,
        out_shape=jax.ShapeDtypeStruct((B, L, MLA_HEADS * V_HEAD), BF16),
        scratch_shapes=[pltpu.VMEM((L, 2 * V_HEAD), BF16),
                        pltpu.VMEM((ATTN_TILE, n_tiles * ATTN_TILE), F32),
                        pltpu.VMEM((ATTN_TILE, n_tiles * ATTN_TILE), BF16),
                        pltpu.VMEM((L, 2 * V_HEAD), F32)],
        compiler_params=_cparams(("parallel", "parallel")),
        name="attention",
    )(q, k, v)


def _seq_kernel(xp_ref, mqk_ref, mv_ref, mo_ref, zs_ref, pw_ref, ps_ref,
                cw_ref, cb_ref, gbc_ref, mn_ref, yp_ref, ym_ref, ct_ref, m_ref, *, n_full, n_batch):
    HIST = 16
    L = xp_ref.shape[1]
    lane = lax.broadcasted_iota(jnp.int32, (1, LANE), 1)
    lane_lo = lane < MLSTM_HEAD_DIM
    lane256 = lax.broadcasted_iota(jnp.int32, (1, POOL_WIDTH), 1)
    grp = lane256 // POOL_GROUP_DIM
    win = jnp.left_shift(2, grp).astype(F32)
    row8 = lax.broadcasted_iota(jnp.int32, (2 * MLSTM_HEADS, 1), 0)

    ct_ref[...] = jnp.zeros_like(ct_ref)
    m_ref[...] = jnp.zeros_like(m_ref)

    def chunk(T, bb, gcol_raw, xx_p, xx_c, mv_c, mo_c, pos0, store, update_state=True):
        ti = lax.broadcasted_iota(jnp.int32, (T, T), 0)
        si = lax.broadcasted_iota(jnp.int32, (T, T), 1)
        causal = si <= ti
        tril = causal.astype(F32)
        triu = (ti <= si).astype(F32)
        t_col = lax.broadcasted_iota(jnp.int32, (T, 1), 0)
        s2 = xx_p + pltpu.roll(xx_p, 1, 0)
        s4 = s2 + pltpu.roll(s2, 2, 0)
        s8 = s4 + pltpu.roll(s4, 4, 0)
        s16 = s8 + pltpu.roll(s8, 8, 0)
        ssum = jnp.where(grp == 0, s2[HIST:], jnp.where(grp == 1, s4[HIST:],
                         jnp.where(grp == 2, s8[HIST:], s16[HIST:])))
        cnt = jnp.maximum(pos0 + t_col + 1, 1).astype(F32)
        d = ssum / jnp.minimum(cnt, win) - xx_p[HIST:]
        yp = _dot(d.astype(BF16), pw_ref[...]) * ps_ref[...]

        cw = cw_ref[...]
        y = (xx_c * cw[3:4] + pltpu.roll(xx_c, 1, 0) * cw[2:3] + pltpu.roll(xx_c, 2, 0) * cw[1:2]
             + pltpu.roll(xx_c, 3, 0) * cw[0:1])[HIST:] + cb_ref[...]
        qk = y * jax.nn.sigmoid(y)
        qc = qk[:, :MLSTM_WIDTH]
        kc = qk[:, MLSTM_WIDTH:] * (MLSTM_HEAD_DIM ** -0.5)

        gcol = gcol_raw + gbc_ref[...]
        grow = gcol.T[0:2 * MLSTM_HEADS, :]
        is_f = jnp.logical_and(lane >= MLSTM_HEADS, lane < 2 * MLSTM_HEADS)
        b_col = _dot_f32(tril, jnp.where(is_f, _log_sigmoid(gcol), 0.0))
        b_row = _dot_f32(jnp.where(row8 >= MLSTM_HEADS, _log_sigmoid(grow), 0.0), triu)

        ym = []
        for pr in range(MLSTM_HEADS // 2):
            psl = slice(LANE * pr, LANE * (pr + 1))
            q_pair, k_pair = qc[:, psl], kc[:, psl]
            v_pair = mv_c[:, psl]
            h_heads = []
            for e in range(2):
                hd = 2 * pr + e
                own = lane_lo if e == 0 else jnp.logical_not(lane_lo)
                den_lane = MLSTM_HEAD_DIM if e == 0 else 0
                q_h = jnp.where(own, q_pair, 0.0).astype(BF16)
                k_h = jnp.where(own, k_pair, 0.0).astype(BF16)
                v_aug = jnp.where(own, v_pair, jnp.where(lane == den_lane, 1.0, 0.0))
                bc = b_col[:, MLSTM_HEADS + hd:MLSTM_HEADS + hd + 1]
                lic = gcol[:, hd:hd + 1]
                br = b_row[MLSTM_HEADS + hd:MLSTM_HEADS + hd + 1, :]
                lir = grow[hd:hd + 1, :]
                dm = jnp.where(causal, bc + (lir - br), NEG_INF)
                m_prev = m_ref[bb, hd][:, 0:1]
                m_inter = bc + m_prev
                mt = jnp.maximum(m_inter, jnp.max(dm, axis=-1, keepdims=True))
                s = _dot_nt(q_h, k_h) * jnp.exp(dm - mt)
                decay = jnp.exp(m_inter - mt)
                num = (_dot(s.astype(BF16), v_aug.astype(BF16))
                       + decay * _dot(q_h, ct_ref[bb, hd].astype(BF16)))
                den = num[:, den_lane:den_lane + 1]
                h_heads.append(num / jnp.maximum(jnp.abs(den), jnp.exp(-mt)))
                if not update_state:
                    continue
                b_last = bc[T - 1:T, :]
                g_col = b_last - bc + lic
                m_new = jnp.maximum(b_last + m_prev, jnp.max(g_col, axis=0, keepdims=True))
                wg = jnp.exp(g_col - m_new)
                cd = jnp.exp(b_last + m_prev - m_new)
                ct_ref[bb, hd] = cd * ct_ref[bb, hd] + _dot_tn(k_h, (v_aug * wg).astype(BF16))
                m_ref[bb, hd] = jnp.broadcast_to(m_new, (1, LANE))
            h_pair = jnp.where(lane_lo, h_heads[0], h_heads[1])
            sq = h_pair * h_pair
            ms = jnp.where(lane_lo,
                           jnp.sum(jnp.where(lane_lo, sq, 0.0), axis=-1, keepdims=True),
                           jnp.sum(jnp.where(lane_lo, 0.0, sq), axis=-1, keepdims=True)) / MLSTM_HEAD_DIM
            hn = h_pair * lax.rsqrt(ms + EPS) * mn_ref[:, psl]
            ym.append(hn * jax.nn.sigmoid(mo_c[:, psl]))
        store(yp, jnp.concatenate(ym, axis=-1))

    T = SEQ_CHUNK
    TT = SEQ_TAIL
    tail = L - n_full * T

    def store_rows(bb, rows, n=None):
        def store(yp, ym):
            yp_ref[bb, rows, :] = (yp if n is None else yp[:n]).astype(yp_ref.dtype)
            ym_ref[bb, rows, :] = (ym if n is None else ym[:n]).astype(ym_ref.dtype)
        return store

    def head(ref, bb, width):
        return jnp.concatenate([jnp.zeros((HIST, width), F32), ref[bb, 0:T, :].astype(F32)], axis=0)

    for bb in range(n_batch):
        chunk(T, bb, zs_ref[bb, 0:T, :], head(xp_ref, bb, POOL_WIDTH),
              head(mqk_ref, bb, 2 * MLSTM_WIDTH), mv_ref[bb, 0:T, :].astype(F32),
              mo_ref[bb, 0:T, :].astype(F32), 0, store_rows(bb, slice(0, T)))

    def body(c, carry):
        r0 = pl.multiple_of(T * c, T)
        ext = pl.ds(pl.multiple_of(r0 - HIST, 16), T + HIST)
        cur = pl.ds(r0, T)
        for bb in range(n_batch):
            chunk(T, bb, zs_ref[bb, cur, :], xp_ref[bb, ext, :].astype(F32),
                  mqk_ref[bb, ext, :].astype(F32), mv_ref[bb, cur, :].astype(F32),
                  mo_ref[bb, cur, :].astype(F32), r0, store_rows(bb, cur))
        return carry

    lax.fori_loop(1, n_full, body, 0)

    r0 = n_full * T

    def padded(ref, bb, width, lo):
        return jnp.concatenate([ref[bb, lo:L, :].astype(F32), jnp.zeros((TT - tail, width), F32)], axis=0)

    for bb in range(n_batch):
        closed = jnp.broadcast_to(jnp.where(lane < MLSTM_HEADS, NEG_INF, 1e4), (TT - tail, LANE))
        chunk(TT, bb, jnp.concatenate([zs_ref[bb, r0:L, :], closed], axis=0),
              padded(xp_ref, bb, POOL_WIDTH, r0 - HIST),
              padded(mqk_ref, bb, 2 * MLSTM_WIDTH, r0 - HIST), padded(mv_ref, bb, MLSTM_WIDTH, r0),
              padded(mo_ref, bb, MLSTM_WIDTH, r0), r0, store_rows(bb, slice(r0, L), tail),
              update_state=False)


def _seq_mixer(xp, mqk, mv, mo, zs, pw, ps, cw, cb, gbc, mn):
    B, L, _ = xp.shape
    n_full = L // SEQ_CHUNK
    assert 0 < L - n_full * SEQ_CHUNK <= SEQ_TAIL
    BB = SEQ_BATCH
    assert B % BB == 0
    seq = lambda c: pl.BlockSpec((BB, L, c), lambda b: (b, 0, 0))
    full = lambda a: pl.BlockSpec(a.shape, lambda b: (0,) * a.ndim)
    return pl.pallas_call(
        functools.partial(_seq_kernel, n_full=n_full, n_batch=BB),
        grid=(B // BB,),
        in_specs=[seq(POOL_WIDTH), seq(2 * MLSTM_WIDTH), seq(MLSTM_WIDTH), seq(MLSTM_WIDTH),
                  seq(LANE), full(pw), full(ps), full(cw), full(cb), full(gbc), full(mn)],
        out_specs=[seq(POOL_WIDTH), seq(MLSTM_WIDTH)],
        out_shape=[jax.ShapeDtypeStruct((B, L, POOL_WIDTH), BF16),
                   jax.ShapeDtypeStruct((B, L, MLSTM_WIDTH), BF16)],
        scratch_shapes=[pltpu.VMEM((BB, MLSTM_HEADS, LANE, LANE), F32),
                        pltpu.VMEM((BB, MLSTM_HEADS, 1, LANE), F32)],
        compiler_params=_cparams(("parallel",)),
        name="seq_mixer",
    )(xp, mqk, mv, mo, zs, pw, ps, cw, cb, gbc, mn)


def _ffn_kernel(h_ref, ya_ref, yp_ref, ym_ref, wo_ref, g_ref, wgu_ref, wd_ref,
                o_ref, xn_ref, hid_ref):
    hf = h_ref[...] + _dot(jnp.concatenate([ya_ref[...], yp_ref[...], ym_ref[...]], axis=1), wo_ref[...])
    o_ref[...] = hf
    xn_ref[...] = _rms(hf, g_ref[...]).astype(BF16)
    ff = hid_ref.shape[1]
    ft = FF_TILE_DENSE
    for c0 in range(0, ff, ft):
        au = _dot(xn_ref[...], wgu_ref[:, 2 * c0:2 * (c0 + ft)])
        a, u = au[:, :ft], au[:, ft:]
        hid_ref[:, c0:c0 + ft] = (a * jax.nn.sigmoid(a) * u).astype(BF16)
    o_ref[...] += _dot(hid_ref[...], wd_ref[...])


def _ffn(h, ya, yp, ym, wo, g, wgu, wd):
    N, D = h.shape
    FF = wd.shape[0]
    T = _pick_tile(N, FFN_ROW_TILE)
    assert FF % FF_TILE_DENSE == 0
    row = lambda a: pl.BlockSpec((T, a.shape[1]), lambda i: (i, 0))
    full = lambda a: pl.BlockSpec(a.shape, lambda i: (0, 0), pipeline_mode=pl.Buffered(1))
    return pl.pallas_call(
        _ffn_kernel,
        grid=(N // T,),
        in_specs=[row(h), row(ya), row(yp), row(ym), full(wo), full(g),
                  full(wgu), full(wd)],
        out_specs=row(h),
        out_shape=jax.ShapeDtypeStruct((N, D), F32),
        scratch_shapes=[pltpu.VMEM((T, D), BF16), pltpu.VMEM((T, FF), BF16)],
        compiler_params=_cparams(("parallel",)),
        name="dense_ffn",
    )(h, ya, yp, ym, wo, g, wgu, wd)


def _router_kernel(h_ref, ya_ref, yp_ref, ym_ref, wo_ref, g_ref, wrh_ref, wrl_ref, tri_ref,
                   hf_ref, route_ref, cnt_ref, run_ref):
    i = pl.program_id(0)
    T = h_ref.shape[0]

    @pl.when(i == 0)
    def _():
        run_ref[...] = jnp.zeros_like(run_ref)

    hf = h_ref[...] + _dot(jnp.concatenate([ya_ref[...], yp_ref[...], ym_ref[...]], axis=1), wo_ref[...])
    hf_ref[...] = hf
    xn = _rms(hf, g_ref[...])
    lane = lax.broadcasted_iota(jnp.int32, (T, LANE), 1).astype(F32)
    x_hi = xn.astype(BF16)
    x_lo = (xn - x_hi.astype(F32)).astype(BF16)
    logits = _dot(x_hi, wrh_ref[...]) + (_dot(x_lo, wrh_ref[...]) + _dot(x_hi, wrl_ref[...]))
    logits = jnp.where(lane < N_EXPERTS, logits, NEG_INF)
    v1 = jnp.max(logits, axis=-1, keepdims=True)
    i1 = jnp.min(jnp.where(logits == v1, lane, float(LANE)), axis=-1, keepdims=True)
    rest = jnp.where(lane == i1, NEG_INF, logits)
    v2 = jnp.max(rest, axis=-1, keepdims=True)
    i2 = jnp.min(jnp.where(rest == v2, lane, float(LANE)), axis=-1, keepdims=True)
    e2 = jnp.exp(v2 - v1)
    g1 = 1.0 / (1.0 + e2)
    g2 = e2 / (1.0 + e2)
    member = jnp.logical_or(lane == i1, lane == i2).astype(F32)
    rank = run_ref[0:1, :] + _dot(tri_ref[...], member.astype(BF16))
    r1 = jnp.sum(jnp.where(lane == i1, rank, 0.0), axis=-1, keepdims=True)
    r2 = jnp.sum(jnp.where(lane == i2, rank, 0.0), axis=-1, keepdims=True)
    run_ref[...] = run_ref[...] + jnp.sum(member, axis=0, keepdims=True)
    cnt_ref[...] = run_ref[...]
    route = jnp.where(lane == 0, i1, jnp.where(lane == 1, i2,
            jnp.where(lane == 2, r1, jnp.where(lane == 3, r2,
            jnp.where(lane == 4, g1, jnp.where(lane == 5, g2, 0.0))))))
    route_ref[...] = route


def _router(h, ya, yp, ym, wo, g, wr):
    N, D = h.shape
    T = _pick_tile(N, FLAT_TILE)
    tri = jnp.tril(jnp.ones((T, T), BF16), -1)
    wr_hi = wr.astype(BF16)
    wr_lo = (wr - wr_hi.astype(F32)).astype(BF16)
    row = lambda a: pl.BlockSpec((T, a.shape[1]), lambda i: (i, 0))
    full = lambda a: pl.BlockSpec(a.shape, lambda i: (0, 0))
    return pl.pallas_call(
        _router_kernel,
        grid=(N // T,),
        in_specs=[row(h), row(ya), row(yp), row(ym), full(wo), full(g), full(wr_hi),
                  full(wr_lo), full(tri)],
        out_specs=[pl.BlockSpec((T, D), lambda i: (i, 0)),
                   pl.BlockSpec((T, LANE), lambda i: (i, 0)),
                   pl.BlockSpec((8, LANE), lambda i: (0, 0))],
        out_shape=[jax.ShapeDtypeStruct((N, D), F32), jax.ShapeDtypeStruct((N, LANE), F32),
                   jax.ShapeDtypeStruct((8, LANE), F32)],
        scratch_shapes=[pltpu.VMEM((8, LANE), F32)],
        compiler_params=_cparams(("arbitrary",)),
        name="router",
    )(h, ya, yp, ym, wo, g, wr_hi, wr_lo, tri)


def _dispatch_kernel(dest_ref, h_ref, xs_in_ref, xs_ref, sem):
    del xs_in_ref
    n = h_ref.shape[0]

    def copy(t, slot):
        return pltpu.make_async_copy(h_ref.at[pl.ds(t, 1)],
                                     xs_ref.at[pl.ds(dest_ref[0, 0, 2 * t + slot], 1)], sem)

    def issue(t, c):
        copy(t, 0).start()
        copy(t, 1).start()
        return c

    lax.fori_loop(0, n, issue, 0, unroll=8)

    def drain(t, c):
        copy(t, 0).wait()
        copy(t, 1).wait()
        return c

    lax.fori_loop(0, n, drain, 0, unroll=8)


def _dispatch(dest3, h, xs_init):
    n_tiles = dest3.shape[0]
    return pl.pallas_call(
        _dispatch_kernel,
        grid=(n_tiles,),
        in_specs=[pl.BlockSpec((1, 1, dest3.shape[2]), lambda i: (i, 0, 0), memory_space=pltpu.SMEM),
                  pl.BlockSpec((dest3.shape[2] // 2, h.shape[1]), lambda i: (i, 0)),
                  pl.BlockSpec(memory_space=pl.ANY)],
        out_specs=pl.BlockSpec(memory_space=pl.ANY),
        out_shape=jax.ShapeDtypeStruct(xs_init.shape, xs_init.dtype),
        scratch_shapes=[pltpu.SemaphoreType.DMA(())],
        input_output_aliases={2: 0},
        compiler_params=_cparams(("arbitrary",)),
        name="dispatch",
    )(dest3, h, xs_init)


def _expert_kernel(be_ref, x_ref, g_ref, wgu_ref, wd_ref, o_ref, xn_ref):
    r = pl.program_id(0)
    j = pl.program_id(1)
    n_active = be_ref[be_ref.shape[0] - 1]

    @pl.when(r < n_active)
    def _():
        @pl.when(j == 0)
        def _():
            xn_ref[...] = _rms(x_ref[...], g_ref[...]).astype(BF16)

        tf = wd_ref.shape[1]
        au = _dot(xn_ref[...], wgu_ref[0])
        a, u = au[:, :tf], au[:, tf:]
        y = _dot((a * jax.nn.sigmoid(a) * u).astype(BF16), wd_ref[0])

        @pl.when(j == 0)
        def _():
            o_ref[...] = y

        @pl.when(j > 0)
        def _():
            o_ref[...] += y

    @pl.when(jnp.logical_and(r >= n_active, j == 0))
    def _():
        o_ref[...] = jnp.zeros_like(o_ref)


def _experts(block_exp, xs, g, wgu, wd):
    R, D = xs.shape
    TM = MOE_TILE
    FF = wd.shape[1]
    TF = FF_TILE_EXPERT
    assert FF % TF == 0 and R % TM == 0
    grid_spec = pltpu.PrefetchScalarGridSpec(
        num_scalar_prefetch=1,
        grid=(R // TM, FF // TF),
        in_specs=[pl.BlockSpec((TM, D), lambda r, j, be: (r, 0)),
                  pl.BlockSpec(g.shape, lambda r, j, be: (0, 0)),
                  pl.BlockSpec((1, D, 2 * TF), lambda r, j, be: (be[r], 0, j)),
                  pl.BlockSpec((1, TF, D), lambda r, j, be: (be[r], j, 0))],
        out_specs=pl.BlockSpec((TM, D), lambda r, j, be: (r, 0)),
        scratch_shapes=[pltpu.VMEM((TM, D), BF16)],
    )
    return pl.pallas_call(
        _expert_kernel,
        grid_spec=grid_spec,
        out_shape=jax.ShapeDtypeStruct((R, D), F32),
        compiler_params=_cparams(("parallel", "arbitrary")),
        name="experts",
    )(block_exp, xs, g, wgu, wd)


def _combine_kernel(dest_ref, h_ref, route_ref, ys_ref, o_ref, buf_ref, sem):
    T = h_ref.shape[0]

    def copy(t, slot):
        return pltpu.make_async_copy(ys_ref.at[pl.ds(dest_ref[0, 0, 2 * t + slot], 1)],
                                     buf_ref.at[slot, pl.ds(t, 1)], sem)

    def issue(t, c):
        copy(t, 0).start()
        copy(t, 1).start()
        return c

    lax.fori_loop(0, T, issue, 0, unroll=8)

    def drain(t, c):
        copy(t, 0).wait()
        copy(t, 1).wait()
        return c

    lax.fori_loop(0, T, drain, 0, unroll=8)
    route = route_ref[...]
    o_ref[...] = h_ref[...] + route[:, 4:5] * buf_ref[0] + route[:, 5:6] * buf_ref[1]


def _combine(dest3, h, route, ys):
    N, D = h.shape
    T = dest3.shape[2] // 2
    return pl.pallas_call(
        _combine_kernel,
        grid=(N // T,),
        in_specs=[pl.BlockSpec((1, 1, 2 * T), lambda i: (i, 0, 0), memory_space=pltpu.SMEM),
                  pl.BlockSpec((T, D), lambda i: (i, 0)),
                  pl.BlockSpec((T, LANE), lambda i: (i, 0)),
                  pl.BlockSpec(memory_space=pl.ANY)],
        out_specs=pl.BlockSpec((T, D), lambda i: (i, 0)),
        out_shape=jax.ShapeDtypeStruct((N, D), F32),
        scratch_shapes=[pltpu.VMEM((2, T, D), F32), pltpu.SemaphoreType.DMA(())],
        compiler_params=_cparams(("arbitrary",)),
        name="combine",
    )(dest3, h, route, ys)


def _moe(mixed, g, wr, wgu, wd):
    N, D = mixed[0].shape
    T = _pick_tile(N, FLAT_TILE)
    h, route, counts = _router(*mixed, g, wr)
    cnt = counts[0, :N_EXPERTS].astype(jnp.int32)
    padded = (cnt + MOE_TILE - 1) // MOE_TILE * MOE_TILE
    pend = jnp.cumsum(padded)
    pstart = pend - padded
    idx = route[:, 0:2].astype(jnp.int32)
    dest = pstart[idx] + route[:, 2:4].astype(jnp.int32)
    dest3 = dest.reshape(N // T, 1, 2 * T)
    n_blocks = -(-2 * N // MOE_TILE) + N_EXPERTS
    block_exp = jnp.minimum(jnp.searchsorted(pend, jnp.arange(n_blocks) * MOE_TILE, side='right'),
                            N_EXPERTS - 1).astype(jnp.int32)
    block_exp = jnp.concatenate([block_exp, (pend[-1:] // MOE_TILE).astype(jnp.int32)])
    xs = _dispatch(dest3, h, jnp.zeros((n_blocks * MOE_TILE, D), F32))
    ys = _experts(block_exp, xs, g, wgu, wd)
    return _combine(dest3, h, route, ys)


def _final_kernel(h_ref, g_ref, o_ref):
    o_ref[0] = _rms(h_ref[0, N_META:, :], g_ref[...])


def _final(h, g):
    B, L, D = h.shape
    return pl.pallas_call(
        _final_kernel,
        grid=(B,),
        in_specs=[pl.BlockSpec((1, L, D), lambda b: (b, 0, 0)), pl.BlockSpec(g.shape, lambda b: (0, 0))],
        out_specs=pl.BlockSpec((1, L - N_META, D), lambda b: (b, 0, 0)),
        out_shape=jax.ShapeDtypeStruct((B, L - N_META, D), F32),
        compiler_params=_cparams(("parallel",)),
        name="final_norm",
    )(h, g)


def _rope_tables(L):
    pos = jnp.arange(L, dtype=jnp.int32)
    inv_freq = ROPE_THETA ** (-jnp.arange(0, QK_ROPE, 2, dtype=F32) / QK_ROPE)
    ang = pos.astype(F32)[:, None] * inv_freq[None, :]
    cos, sin = jnp.cos(ang), jnp.sin(ang)
    z = lambda n: jnp.zeros((L, n), F32)
    scale = (QK_NOPE + QK_ROPE) ** -0.5 * float(np.log2(np.e))
    cq = jnp.concatenate([jnp.ones((L, QK_NOPE), F32), cos, cos, z(32)], axis=1) * scale
    s1q = jnp.concatenate([z(QK_NOPE), -sin, z(48)], axis=1) * scale
    s2q = jnp.concatenate([z(QK_NOPE + 16), sin, z(32)], axis=1) * scale
    g = 2 * MLSTM_HEADS
    ck = jnp.concatenate([z(g), cos, cos, z(96 - g)], axis=1)
    s1k = jnp.concatenate([z(g), -sin, z(112 - g)], axis=1)
    s2k = jnp.concatenate([z(g + 16), sin, z(96 - g)], axis=1)
    return jnp.concatenate([cq, s1q, s2q, ck, s1k, s2k], axis=1)


def _layer_weights(l, w_in, w_q_up, w_kv_up, pool_w, w_out):
    pts = np.cumsum([Q_LORA, KV_LORA, QK_ROPE, POOL_WIDTH, 2 * MLSTM_WIDTH, MLSTM_WIDTH, MLSTM_WIDTH]).tolist()
    wi = w_in[l]
    w_cq, w_ckv, w_kr, w_xp, w_mqk, w_mv, w_mo, w_mg = jnp.split(wi, pts, axis=1)
    small = jnp.concatenate([w_mg, w_kr, jnp.zeros((D_MODEL, LANE - QK_ROPE - 2 * MLSTM_HEADS), F32)], axis=1)
    wbig = jnp.concatenate([w_cq, w_ckv, w_xp, w_mqk, w_mv, w_mo, small], axis=1).astype(BF16)
    wq = w_q_up[l].reshape(Q_LORA, MLA_HEADS, QK_NOPE + QK_ROPE)
    wq = jnp.pad(wq, ((0, 0), (0, 0), (0, HEAD_PAD - QK_NOPE - QK_ROPE))).reshape(Q_LORA, -1).astype(BF16)
    wkv = w_kv_up[l].reshape(KV_LORA, MLA_HEADS, QK_NOPE + V_HEAD)
    wk = jnp.pad(wkv[:, :, :QK_NOPE], ((0, 0), (0, 0), (0, HEAD_PAD - QK_NOPE))).reshape(KV_LORA, -1).astype(BF16)
    wv = wkv[:, :, QK_NOPE:].reshape(KV_LORA, -1).astype(BF16)
    pw = jax.scipy.linalg.block_diag(*[pool_w[l, g] for g in range(POOL_GROUPS)]).astype(BF16)
    return dict(wbig=wbig, wq=wq, wk=wk, wv=wv, pw=pw, wo=w_out[l].astype(BF16))


def kernel(x, meta_tokens, attn_norm, w_in, q_norm, w_q_up, kv_norm, w_kv_up, pool_w, pool_scale, conv_w, conv_b, gate_bias, mlstm_norm, w_out, ffn_norm, dense_w_gate, dense_w_up, dense_w_down, router_w, moe_w_gate, moe_w_up, moe_w_down, final_norm):
    B, S, D = x.shape
    L = N_META + S
    N = B * L
    depth = w_in.shape[0]
    n_full = L // SEQ_CHUNK
    assert S % ATTN_TILE == 0 and S % SEQ_CHUNK == 0

    meta = jnp.broadcast_to(meta_tokens[None].astype(x.dtype), (B, N_META, D))
    h = jnp.concatenate([meta, x], axis=1)
    tab = _rope_tables(L)
    row = lambda a: a.reshape(1, -1).astype(F32)

    for l in range(depth):
        lw = _layer_weights(l, w_in, w_q_up, w_kv_up, pool_w, w_out)
        q, k, v, xp, mqk, mv, mo, zs = _inproj(h, row(attn_norm[l]), lw['wbig'], row(q_norm[l]), lw['wq'],
                                               row(kv_norm[l]), lw['wk'], lw['wv'], tab)
        ya = _attention(q, k, v)
        gbc = jnp.pad(gate_bias[l], (0, LANE - 2 * MLSTM_HEADS)).reshape(1, LANE).astype(F32)
        yp, ym = _seq_mixer(xp, mqk, mv, mo, zs, lw['pw'], row(pool_scale[l]),
                            conv_w[l].astype(F32), row(conv_b[l]), gbc, row(mlstm_norm[l]))
        mixed = (h.reshape(N, D), ya.reshape(N, -1), yp.reshape(N, -1), ym.reshape(N, -1),
                 lw['wo'])
        j = l // 2
        if l % 2 == 0:
            tiles = (D, D_FF // FF_TILE_DENSE, FF_TILE_DENSE)
            wgu = jnp.concatenate([dense_w_gate[j].astype(BF16).reshape(tiles),
                                   dense_w_up[j].astype(BF16).reshape(tiles)], axis=2)
            hf = _ffn(*mixed, row(ffn_norm[l]), wgu.reshape(D, 2 * D_FF), dense_w_down[j].astype(BF16))
        else:
            wr = jnp.pad(router_w[j], ((0, 0), (0, LANE - N_EXPERTS))).astype(F32)
            tiles = (N_EXPERTS, D, D_FF_EXPERT // FF_TILE_EXPERT, FF_TILE_EXPERT)
            wgu = jnp.concatenate([moe_w_gate[j].astype(BF16).reshape(tiles),
                                   moe_w_up[j].astype(BF16).reshape(tiles)], axis=3)
            hf = _moe(mixed, row(ffn_norm[l]), wr, wgu.reshape(N_EXPERTS, D, 2 * D_FF_EXPERT),
                      moe_w_down[j].astype(BF16))
        h = hf.reshape(B, L, D)
    return _final(h, row(final_norm))
```

```python
import functools

import numpy as np
import jax
import jax.numpy as jnp
from jax import lax
from jax.experimental import pallas as pl
from jax.experimental.pallas import tpu as pltpu

F32 = jnp.float32
BF16 = jnp.bfloat16

D_MODEL = 1024
N_META = 16
CHUNK = 64
MLA_HEADS = 8
QK_NOPE = 64
QK_ROPE = 32
V_HEAD = 64
Q_LORA = 768
KV_LORA = 256
ROPE_THETA = 10000.0
POOL_GROUPS = 4
POOL_GROUP_DIM = 64
POOL_WIDTH = 256
MLSTM_HEADS = 4
MLSTM_HEAD_DIM = 64
MLSTM_WIDTH = 256
CONV_WIDTH = 4
D_FF = 2816
N_EXPERTS = 8
D_FF_EXPERT = 3584
EPS = 1e-6
NEG_INF = -1e30

LANE = 128
HEAD_PAD = 128
ATTN_TILE = 256
SEQ_CHUNK = 256
SEQ_TAIL = 128
SEQ_BATCH = 2
FLAT_TILE = 512
FFN_ROW_TILE = 768
MOE_TILE = 1024
ZERO_ROWS = 256
FF_TILE_DENSE = 1408
FF_TILE_EXPERT = 896
VMEM_LIMIT = 56 * 1024 * 1024

SEG_CQ = (0, 768)
SEG_CKV = (768, 1024)
SEG_XP = (1024, 1280)
SEG_MQK = (1280, 1792)
SEG_MV = (1792, 2048)
SEG_MO = (2048, 2304)
SEG_SMALL = (2304, 2432)
W_BIG = 2432


def _rms(x, g):
    ms = jnp.mean(x * x, axis=-1, keepdims=True)
    return x * lax.rsqrt(ms + EPS) * g


def _dot(a, b):
    return jnp.dot(a, b, preferred_element_type=F32)


def _dot_nt(a, b):
    return lax.dot_general(a, b, (((1,), (1,)), ((), ())), preferred_element_type=F32)


def _dot_tn(a, b):
    return lax.dot_general(a, b, (((0,), (0,)), ((), ())), preferred_element_type=F32)


def _dot_f32(a, b):
    return jnp.dot(a, b, preferred_element_type=F32, precision=lax.Precision.HIGHEST)


def _log_sigmoid(x):
    return jnp.minimum(x, 0.0) - jnp.log(1.0 + jnp.exp(-jnp.abs(x)))


def _pick_tile(n, target):
    best = None
    for t in range(16, target + 1, 16):
        if n % t == 0:
            best = t
    assert best is not None, (n, target)
    return best


def _cparams(sem):
    return pltpu.CompilerParams(dimension_semantics=sem, vmem_limit_bytes=VMEM_LIMIT)


def _rope_lanes(x, c, s_lo, s_hi):
    return x * c + pltpu.roll(x, LANE - 16, 1) * s_lo + pltpu.roll(x, 16, 1) * s_hi


def _inproj_kernel(h_ref, an_ref, wbig_ref, qn_ref, wq_ref, kvn_ref, wk_ref, wv_ref, tab_ref,
                   q_ref, k_ref, v_ref, xp_ref, mqk_ref, mv_ref, mo_ref, zs_ref):
    xn = _rms(h_ref[0], an_ref[...]).astype(BF16)

    z = _dot(xn, wbig_ref[...])

    def seg(ab):
        return z[:, ab[0]:ab[1]]

    xp_ref[0] = seg(SEG_XP).astype(BF16)
    mqk_ref[0] = seg(SEG_MQK).astype(BF16)
    mv_ref[0] = seg(SEG_MV).astype(BF16)
    mo_ref[0] = seg(SEG_MO).astype(BF16)
    zs = seg(SEG_SMALL)
    zs_ref[0] = zs

    tab = tab_ref[...]
    cq, s1q, s2q = tab[:, 0:128], tab[:, 128:256], tab[:, 256:384]
    ck, s1k, s2k = tab[:, 384:512], tab[:, 512:640], tab[:, 640:768]

    cqn = _rms(seg(SEG_CQ), qn_ref[...]).astype(BF16)
    qf = _dot(cqn, wq_ref[...])
    for hd in range(MLA_HEADS):
        sl = slice(HEAD_PAD * hd, HEAD_PAD * (hd + 1))
        q_ref[0, hd] = _rope_lanes(qf[:, sl], cq, s1q, s2q).astype(BF16)

    ckvn = _rms(seg(SEG_CKV), kvn_ref[...]).astype(BF16)
    v_ref[0] = _dot(ckvn, wv_ref[...]).astype(BF16)
    kf = _dot(ckvn, wk_ref[...])
    kr = pltpu.roll(_rope_lanes(zs, ck, s1k, s2k), QK_NOPE - 2 * MLSTM_HEADS, 1)
    for hd in range(MLA_HEADS):
        sl = slice(HEAD_PAD * hd, HEAD_PAD * (hd + 1))
        k_ref[0, hd] = (kf[:, sl] + kr).astype(BF16)


def _inproj(h, an, wbig, qn, wq, kvn, wk, wv, tab):
    B, L, D = h.shape
    T = _pick_tile(L, 768)
    nt = L // T
    row = lambda c: pl.BlockSpec((1, T, c), lambda b, i: (b, i, 0))
    full = lambda a: pl.BlockSpec(a.shape, lambda b, i: (0,) * a.ndim)
    widths = (MLA_HEADS * V_HEAD, POOL_WIDTH, 2 * MLSTM_WIDTH, MLSTM_WIDTH, MLSTM_WIDTH)
    head_major = jax.ShapeDtypeStruct((B, MLA_HEADS, L, HEAD_PAD), BF16)
    head_spec = pl.BlockSpec((1, MLA_HEADS, T, HEAD_PAD), lambda b, i: (b, 0, i, 0))
    out_shape = [head_major, head_major] + [jax.ShapeDtypeStruct((B, L, c), BF16) for c in widths]
    out_shape.append(jax.ShapeDtypeStruct((B, L, LANE), F32))
    out_specs = [head_spec, head_spec] + [row(c) for c in widths] + [row(LANE)]
    return pl.pallas_call(
        _inproj_kernel,
        grid=(B, nt),
        in_specs=[row(D), full(an), full(wbig), full(qn), full(wq), full(kvn), full(wk), full(wv),
                  pl.BlockSpec((T, tab.shape[1]), lambda b, i: (i, 0))],
        out_specs=out_specs,
        out_shape=out_shape,
        compiler_params=_cparams(("parallel", "parallel")),
        name="inproj",
    )(h, an, wbig, qn, wq, kvn, wk, wv, tab)


def _attn_kernel(q_ref, k_ref, v_ref, o_ref, vm_ref, s_ref, p_ref, oacc_ref, *, n_tiles):
    A = ATTN_TILE
    lane = lax.broadcasted_iota(jnp.int32, (1, LANE), 1)
    ri = lax.broadcasted_iota(jnp.int32, (A, A), 0) // CHUNK
    ci = lax.broadcasted_iota(jnp.int32, (A, A), 1) // CHUNK
    diag_mask = ci <= ri
    oacc_ref[...] = jnp.zeros_like(oacc_ref)

    def head_body(hh, carry):
        vm_ref[...] = jnp.where(lane // V_HEAD == hh, v_ref[0], jnp.zeros_like(v_ref[0]))

        s0 = _dot_nt(q_ref[0, hh, 0:N_META, :], k_ref[0, hh, 0:N_META, :])
        p0 = jnp.exp2(s0 - jnp.max(s0, axis=-1, keepdims=True))
        o0 = _dot(p0.astype(BF16), vm_ref[0:N_META, :]) / jnp.sum(p0, axis=-1, keepdims=True)
        oacc_ref[0:N_META, :] += o0

        for j in range(n_tiles):
            r0 = N_META + A * j
            qj = q_ref[0, hh, r0:r0 + A, :]
            s_meta = _dot_nt(qj, k_ref[0, hh, 0:N_META, :])
            m128 = None
            for c in range(j + 1):
                s = _dot_nt(qj, k_ref[0, hh, N_META + A * c:N_META + A * (c + 1), :])
                if c == j:
                    s = jnp.where(diag_mask, s, NEG_INF)
                s_ref[:, A * c:A * (c + 1)] = s
                mc = jnp.maximum(s[:, :LANE], s[:, LANE:])
                m128 = mc if m128 is None else jnp.maximum(m128, mc)
            m = jnp.maximum(jnp.max(m128, axis=-1, keepdims=True),
                            jnp.max(s_meta, axis=-1, keepdims=True))
            p_meta = jnp.exp2(s_meta - m)
            l128 = jnp.zeros((A, LANE), F32)
            for c in range(j + 1):
                p = jnp.exp2(s_ref[:, A * c:A * (c + 1)] - m)
                l128 = l128 + p[:, :LANE] + p[:, LANE:]
                p_ref[:, A * c:A * (c + 1)] = p.astype(BF16)
            l = jnp.sum(p_meta, axis=-1, keepdims=True) + jnp.sum(l128, axis=-1, keepdims=True)
            n = A * (j + 1)
            acc = (_dot(p_ref[:, 0:n], vm_ref[N_META:N_META + n, :])
                   + _dot(p_meta.astype(BF16), vm_ref[0:N_META, :]))
            oacc_ref[r0:r0 + A, :] += acc / l
        return carry

    lax.fori_loop(0, 2, head_body, 0)
    o_ref[0] = oacc_ref[...].astype(o_ref.dtype)


def _attention(q, k, v):
    B, _, L, _ = q.shape
    n_tiles = (L - N_META) // ATTN_TILE
    assert N_META + n_tiles * ATTN_TILE == L
    n_pairs = MLA_HEADS // 2
    qk_spec = pl.BlockSpec((1, 2, L, HEAD_PAD), lambda b, p: (b, p, 0, 0))
    v_spec = pl.BlockSpec((1, L, 2 * V_HEAD), lambda b, p: (b, 0, p))
    return pl.pallas_call(
        functools.partial(_attn_kernel, n_tiles=n_tiles),
        grid=(B, n_pairs),
        in_specs=[qk_spec, qk_spec, v_spec],
        out_specs=v_spec,
        out_shape=jax.ShapeDtypeStruct((B, L, MLA_HEADS * V_HEAD), BF16),
        scratch_shapes=[pltpu.VMEM((L, 2 * V_HEAD), BF16),
                        pltpu.VMEM((ATTN_TILE, n_tiles * ATTN_TILE), F32),
                        pltpu.VMEM((ATTN_TILE, n_tiles * ATTN_TILE), BF16),
                        pltpu.VMEM((L, 2 * V_HEAD), F32)],
        compiler_params=_cparams(("parallel", "parallel")),
        name="attention",
    )(q, k, v)


def _seq_kernel(xp_ref, mqk_ref, mv_ref, mo_ref, zs_ref, pw_ref, ps_ref,
                cw_ref, cb_ref, gbc_ref, mn_ref, yp_ref, ym_ref, ct_ref, m_ref, *, n_full, n_batch):
    HIST = 16
    L = xp_ref.shape[1]
    lane = lax.broadcasted_iota(jnp.int32, (1, LANE), 1)
    lane_lo = lane < MLSTM_HEAD_DIM
    lane256 = lax.broadcasted_iota(jnp.int32, (1, POOL_WIDTH), 1)
    grp = lane256 // POOL_GROUP_DIM
    win = jnp.left_shift(2, grp).astype(F32)
    row8 = lax.broadcasted_iota(jnp.int32, (2 * MLSTM_HEADS, 1), 0)

    ct_ref[...] = jnp.zeros_like(ct_ref)
    m_ref[...] = jnp.zeros_like(m_ref)

    def chunk(T, bb, gcol_raw, xx_p, xx_c, mv_c, mo_c, pos0, store, update_state=True):
        ti = lax.broadcasted_iota(jnp.int32, (T, T), 0)
        si = lax.broadcasted_iota(jnp.int32, (T, T), 1)
        causal = si <= ti
        tril = causal.astype(F32)
        triu = (ti <= si).astype(F32)
        t_col = lax.broadcasted_iota(jnp.int32, (T, 1), 0)
        s2 = xx_p + pltpu.roll(xx_p, 1, 0)
        s4 = s2 + pltpu.roll(s2, 2, 0)
        s8 = s4 + pltpu.roll(s4, 4, 0)
        s16 = s8 + pltpu.roll(s8, 8, 0)
        ssum = jnp.where(grp == 0, s2[HIST:], jnp.where(grp == 1, s4[HIST:],
                         jnp.where(grp == 2, s8[HIST:], s16[HIST:])))
        cnt = jnp.maximum(pos0 + t_col + 1, 1).astype(F32)
        d = ssum / jnp.minimum(cnt, win) - xx_p[HIST:]
        yp = _dot(d.astype(BF16), pw_ref[...]) * ps_ref[...]

        cw = cw_ref[...]
        y = (xx_c * cw[3:4] + pltpu.roll(xx_c, 1, 0) * cw[2:3] + pltpu.roll(xx_c, 2, 0) * cw[1:2]
             + pltpu.roll(xx_c, 3, 0) * cw[0:1])[HIST:] + cb_ref[...]
        qk = y * jax.nn.sigmoid(y)
        qc = qk[:, :MLSTM_WIDTH]
        kc = qk[:, MLSTM_WIDTH:] * (MLSTM_HEAD_DIM ** -0.5)

        gcol = gcol_raw + gbc_ref[...]
        grow = gcol.T[0:2 * MLSTM_HEADS, :]
        is_f = jnp.logical_and(lane >= MLSTM_HEADS, lane < 2 * MLSTM_HEADS)
        b_col = _dot_f32(tril, jnp.where(is_f, _log_sigmoid(gcol), 0.0))
        b_row = _dot_f32(jnp.where(row8 >= MLSTM_HEADS, _log_sigmoid(grow), 0.0), triu)

        ym = []
        for pr in range(MLSTM_HEADS // 2):
            psl = slice(LANE * pr, LANE * (pr + 1))
            q_pair, k_pair = qc[:, psl], kc[:, psl]
            v_pair = mv_c[:, psl]
            h_heads = []
            for e in range(2):
                hd = 2 * pr + e
                own = lane_lo if e == 0 else jnp.logical_not(lane_lo)
                den_lane = MLSTM_HEAD_DIM if e == 0 else 0
                q_h = jnp.where(own, q_pair, 0.0).astype(BF16)
                k_h = jnp.where(own, k_pair, 0.0).astype(BF16)
                v_aug = jnp.where(own, v_pair, jnp.where(lane == den_lane, 1.0, 0.0))
                bc = b_col[:, MLSTM_HEADS + hd:MLSTM_HEADS + hd + 1]
                lic = gcol[:, hd:hd + 1]
                br = b_row[MLSTM_HEADS + hd:MLSTM_HEADS + hd + 1, :]
                lir = grow[hd:hd + 1, :]
                dm = jnp.where(causal, bc + (lir - br), NEG_INF)
                m_prev = m_ref[bb, hd][:, 0:1]
                m_inter = bc + m_prev
                mt = jnp.maximum(m_inter, jnp.max(dm, axis=-1, keepdims=True))
                s = _dot_nt(q_h, k_h) * jnp.exp(dm - mt)
                decay = jnp.exp(m_inter - mt)
                num = (_dot(s.astype(BF16), v_aug.astype(BF16))
                       + decay * _dot(q_h, ct_ref[bb, hd].astype(BF16)))
                den = num[:, den_lane:den_lane + 1]
                h_heads.append(num / jnp.maximum(jnp.abs(den), jnp.exp(-mt)))
                if not update_state:
                    continue
                b_last = bc[T - 1:T, :]
                g_col = b_last - bc + lic
                m_new = jnp.maximum(b_last + m_prev, jnp.max(g_col, axis=0, keepdims=True))
                wg = jnp.exp(g_col - m_new)
                cd = jnp.exp(b_last + m_prev - m_new)
                ct_ref[bb, hd] = cd * ct_ref[bb, hd] + _dot_tn(k_h, (v_aug * wg).astype(BF16))
                m_ref[bb, hd] = jnp.broadcast_to(m_new, (1, LANE))
            h_pair = jnp.where(lane_lo, h_heads[0], h_heads[1])
            sq = h_pair * h_pair
            ms = jnp.where(lane_lo,
                           jnp.sum(jnp.where(lane_lo, sq, 0.0), axis=-1, keepdims=True),
                           jnp.sum(jnp.where(lane_lo, 0.0, sq), axis=-1, keepdims=True)) / MLSTM_HEAD_DIM
            hn = h_pair * lax.rsqrt(ms + EPS) * mn_ref[:, psl]
            ym.append(hn * jax.nn.sigmoid(mo_c[:, psl]))
        store(yp, jnp.concatenate(ym, axis=-1))

    T = SEQ_CHUNK
    TT = SEQ_TAIL
    tail = L - n_full * T

    def store_rows(bb, rows, n=None):
        def store(yp, ym):
            yp_ref[bb, rows, :] = (yp if n is None else yp[:n]).astype(yp_ref.dtype)
            ym_ref[bb, rows, :] = (ym if n is None else ym[:n]).astype(ym_ref.dtype)
        return store

    def head(ref, bb, width):
        return jnp.concatenate([jnp.zeros((HIST, width), F32), ref[bb, 0:T, :].astype(F32)], axis=0)

    for bb in range(n_batch):
        chunk(T, bb, zs_ref[bb, 0:T, :], head(xp_ref, bb, POOL_WIDTH),
              head(mqk_ref, bb, 2 * MLSTM_WIDTH), mv_ref[bb, 0:T, :].astype(F32),
              mo_ref[bb, 0:T, :].astype(F32), 0, store_rows(bb, slice(0, T)))

    def body(c, carry):
        r0 = pl.multiple_of(T * c, T)
        ext = pl.ds(pl.multiple_of(r0 - HIST, 16), T + HIST)
        cur = pl.ds(r0, T)
        for bb in range(n_batch):
            chunk(T, bb, zs_ref[bb, cur, :], xp_ref[bb, ext, :].astype(F32),
                  mqk_ref[bb, ext, :].astype(F32), mv_ref[bb, cur, :].astype(F32),
                  mo_ref[bb, cur, :].astype(F32), r0, store_rows(bb, cur))
        return carry

    lax.fori_loop(1, n_full, body, 0)

    r0 = n_full * T

    def padded(ref, bb, width, lo):
        return jnp.concatenate([ref[bb, lo:L, :].astype(F32), jnp.zeros((TT - tail, width), F32)], axis=0)

    for bb in range(n_batch):
        closed = jnp.broadcast_to(jnp.where(lane < MLSTM_HEADS, NEG_INF, 1e4), (TT - tail, LANE))
        chunk(TT, bb, jnp.concatenate([zs_ref[bb, r0:L, :], closed], axis=0),
              padded(xp_ref, bb, POOL_WIDTH, r0 - HIST),
              padded(mqk_ref, bb, 2 * MLSTM_WIDTH, r0 - HIST), padded(mv_ref, bb, MLSTM_WIDTH, r0),
              padded(mo_ref, bb, MLSTM_WIDTH, r0), r0, store_rows(bb, slice(r0, L), tail),
              update_state=False)


def _seq_mixer(xp, mqk, mv, mo, zs, pw, ps, cw, cb, gbc, mn):
    B, L, _ = xp.shape
    n_full = L // SEQ_CHUNK
    assert 0 < L - n_full * SEQ_CHUNK <= SEQ_TAIL
    BB = SEQ_BATCH
    assert B % BB == 0
    seq = lambda c: pl.BlockSpec((BB, L, c), lambda b: (b, 0, 0))
    full = lambda a: pl.BlockSpec(a.shape, lambda b: (0,) * a.ndim)
    return pl.pallas_call(
        functools.partial(_seq_kernel, n_full=n_full, n_batch=BB),
        grid=(B // BB,),
        in_specs=[seq(POOL_WIDTH), seq(2 * MLSTM_WIDTH), seq(MLSTM_WIDTH), seq(MLSTM_WIDTH),
                  seq(LANE), full(pw), full(ps), full(cw), full(cb), full(gbc), full(mn)],
        out_specs=[seq(POOL_WIDTH), seq(MLSTM_WIDTH)],
        out_shape=[jax.ShapeDtypeStruct((B, L, POOL_WIDTH), BF16),
                   jax.ShapeDtypeStruct((B, L, MLSTM_WIDTH), BF16)],
        scratch_shapes=[pltpu.VMEM((BB, MLSTM_HEADS, LANE, LANE), F32),
                        pltpu.VMEM((BB, MLSTM_HEADS, 1, LANE), F32)],
        compiler_params=_cparams(("parallel",)),
        name="seq_mixer",
    )(xp, mqk, mv, mo, zs, pw, ps, cw, cb, gbc, mn)


def _ffn_kernel(h_ref, ya_ref, yp_ref, ym_ref, wo_ref, g_ref, wgu_ref, wd_ref,
                o_ref, xn_ref, hid_ref):
    hf = h_ref[...] + _dot(jnp.concatenate([ya_ref[...], yp_ref[...], ym_ref[...]], axis=1), wo_ref[...])
    o_ref[...] = hf
    xn_ref[...] = _rms(hf, g_ref[...]).astype(BF16)
    ff = hid_ref.shape[1]
    ft = FF_TILE_DENSE
    for c0 in range(0, ff, ft):
        au = _dot(xn_ref[...], wgu_ref[:, 2 * c0:2 * (c0 + ft)])
        a, u = au[:, :ft], au[:, ft:]
        hid_ref[:, c0:c0 + ft] = (a * jax.nn.sigmoid(a) * u).astype(BF16)
    o_ref[...] += _dot(hid_ref[...], wd_ref[...])


def _ffn(h, ya, yp, ym, wo, g, wgu, wd):
    N, D = h.shape
    FF = wd.shape[0]
    T = _pick_tile(N, FFN_ROW_TILE)
    assert FF % FF_TILE_DENSE == 0
    row = lambda a: pl.BlockSpec((T, a.shape[1]), lambda i: (i, 0))
    full = lambda a: pl.BlockSpec(a.shape, lambda i: (0, 0), pipeline_mode=pl.Buffered(1))
    return pl.pallas_call(
        _ffn_kernel,
        grid=(N // T,),
        in_specs=[row(h), row(ya), row(yp), row(ym), full(wo), full(g),
                  full(wgu), full(wd)],
        out_specs=row(h),
        out_shape=jax.ShapeDtypeStruct((N, D), F32),
        scratch_shapes=[pltpu.VMEM((T, D), BF16), pltpu.VMEM((T, FF), BF16)],
        compiler_params=_cparams(("parallel",)),
        name="dense_ffn",
    )(h, ya, yp, ym, wo, g, wgu, wd)


def _router_kernel(h_ref, ya_ref, yp_ref, ym_ref, wo_ref, g_ref, wrh_ref, wrl_ref, tri_ref,
                   hf_ref, route_ref, cnt_ref, run_ref):
    i = pl.program_id(0)
    T = h_ref.shape[0]

    @pl.when(i == 0)
    def _():
        run_ref[...] = jnp.zeros_like(run_ref)

    hf = h_ref[...] + _dot(jnp.concatenate([ya_ref[...], yp_ref[...], ym_ref[...]], axis=1), wo_ref[...])
    hf_ref[...] = hf
    xn = _rms(hf, g_ref[...])
    lane = lax.broadcasted_iota(jnp.int32, (T, LANE), 1).astype(F32)
    x_hi = xn.astype(BF16)
    x_lo = (xn - x_hi.astype(F32)).astype(BF16)
    logits = _dot(x_hi, wrh_ref[...]) + (_dot(x_lo, wrh_ref[...]) + _dot(x_hi, wrl_ref[...]))
    logits = jnp.where(lane < N_EXPERTS, logits, NEG_INF)
    v1 = jnp.max(logits, axis=-1, keepdims=True)
    i1 = jnp.min(jnp.where(logits == v1, lane, float(LANE)), axis=-1, keepdims=True)
    rest = jnp.where(lane == i1, NEG_INF, logits)
    v2 = jnp.max(rest, axis=-1, keepdims=True)
    i2 = jnp.min(jnp.where(rest == v2, lane, float(LANE)), axis=-1, keepdims=True)
    e2 = jnp.exp(v2 - v1)
    g1 = 1.0 / (1.0 + e2)
    g2 = e2 / (1.0 + e2)
    member = jnp.logical_or(lane == i1, lane == i2).astype(F32)
    rank = run_ref[0:1, :] + _dot(tri_ref[...], member.astype(BF16))
    r1 = jnp.sum(jnp.where(lane == i1, rank, 0.0), axis=-1, keepdims=True)
    r2 = jnp.sum(jnp.where(lane == i2, rank, 0.0), axis=-1, keepdims=True)
    run_ref[...] = run_ref[...] + jnp.sum(member, axis=0, keepdims=True)
    cnt_ref[...] = run_ref[...]
    route = jnp.where(lane == 0, i1, jnp.where(lane == 1, i2,
            jnp.where(lane == 2, r1, jnp.where(lane == 3, r2,
            jnp.where(lane == 4, g1, jnp.where(lane == 5, g2, 0.0))))))
    route_ref[...] = route


def _router(h, ya, yp, ym, wo, g, wr):
    N, D = h.shape
    T = _pick_tile(N, FLAT_TILE)
    tri = jnp.tril(jnp.ones((T, T), BF16), -1)
    wr_hi = wr.astype(BF16)
    wr_lo = (wr - wr_hi.astype(F32)).astype(BF16)
    row = lambda a: pl.BlockSpec((T, a.shape[1]), lambda i: (i, 0))
    full = lambda a: pl.BlockSpec(a.shape, lambda i: (0, 0))
    return pl.pallas_call(
        _router_kernel,
        grid=(N // T,),
        in_specs=[row(h), row(ya), row(yp), row(ym), full(wo), full(g), full(wr_hi),
                  full(wr_lo), full(tri)],
        out_specs=[pl.BlockSpec((T, D), lambda i: (i, 0)),
                   pl.BlockSpec((T, LANE), lambda i: (i, 0)),
                   pl.BlockSpec((8, LANE), lambda i: (0, 0))],
        out_shape=[jax.ShapeDtypeStruct((N, D), F32), jax.ShapeDtypeStruct((N, LANE), F32),
                   jax.ShapeDtypeStruct((8, LANE), F32)],
        scratch_shapes=[pltpu.VMEM((8, LANE), F32)],
        compiler_params=_cparams(("arbitrary",)),
        name="router",
    )(h, ya, yp, ym, wo, g, wr_hi, wr_lo, tri)


def _dispatch_kernel(seg_ref, dest_ref, h_ref, xs_ref, zero_ref, sem):
    n = h_ref.shape[0]
    zb = zero_ref.shape[0]
    n_rows = xs_ref.shape[0]

    def pad_row(r):
        return pltpu.make_async_copy(zero_ref.at[pl.ds(0, 1)], xs_ref.at[pl.ds(r, 1)], sem)

    def tail_block(b):
        start = pl.multiple_of(seg_ref[2 * N_EXPERTS - 1] + b * zb, zb)
        return pltpu.make_async_copy(zero_ref, xs_ref.at[pl.ds(start, zb)], sem)

    def for_unwritten(start):
        for e in range(N_EXPERTS):
            lax.fori_loop(seg_ref[e], seg_ref[N_EXPERTS + e],
                          lambda r, c: (pad_row(r).start() if start else pad_row(r).wait(), c)[1], 0)
        lax.fori_loop(0, (n_rows - seg_ref[2 * N_EXPERTS - 1]) // zb,
                      lambda b, c: (tail_block(b).start() if start else tail_block(b).wait(), c)[1], 0)

    @pl.when(pl.program_id(0) == 0)
    def _():
        zero_ref[...] = jnp.zeros_like(zero_ref)
        for_unwritten(True)
        for_unwritten(False)

    def copy(t, slot):
        return pltpu.make_async_copy(h_ref.at[pl.ds(t, 1)],
                                     xs_ref.at[pl.ds(dest_ref[0, 0, 2 * t + slot], 1)], sem)

    def issue(t, c):
        copy(t, 0).start()
        copy(t, 1).start()
        return c

    lax.fori_loop(0, n, issue, 0, unroll=8)

    def drain(t, c):
        copy(t, 0).wait()
        copy(t, 1).wait()
        return c

    lax.fori_loop(0, n, drain, 0, unroll=8)


def _dispatch(seg, dest3, h, n_rows):
    n_tiles = dest3.shape[0]
    T, D = dest3.shape[2] // 2, h.shape[1]
    grid_spec = pltpu.PrefetchScalarGridSpec(
        num_scalar_prefetch=1,
        grid=(n_tiles,),
        in_specs=[pl.BlockSpec((1, 1, 2 * T), lambda i, seg: (i, 0, 0), memory_space=pltpu.SMEM),
                  pl.BlockSpec((T, D), lambda i, seg: (i, 0))],
        out_specs=pl.BlockSpec(memory_space=pl.ANY),
        scratch_shapes=[pltpu.VMEM((ZERO_ROWS, D), F32), pltpu.SemaphoreType.DMA(())],
    )
    return pl.pallas_call(
        _dispatch_kernel,
        grid_spec=grid_spec,
        out_shape=jax.ShapeDtypeStruct((n_rows, D), F32),
        compiler_params=_cparams(("arbitrary",)),
        name="dispatch",
    )(seg, dest3, h)


def _expert_kernel(be_ref, x_ref, g_ref, wgu_ref, wd_ref, o_ref, xn_ref):
    r = pl.program_id(0)
    j = pl.program_id(1)
    n_active = be_ref[be_ref.shape[0] - 1]

    @pl.when(r < n_active)
    def _():
        @pl.when(j == 0)
        def _():
            xn_ref[...] = _rms(x_ref[...], g_ref[...]).astype(BF16)

        tf = wd_ref.shape[1]
        au = _dot(xn_ref[...], wgu_ref[0])
        a, u = au[:, :tf], au[:, tf:]
        y = _dot((a * jax.nn.sigmoid(a) * u).astype(BF16), wd_ref[0])

        @pl.when(j == 0)
        def _():
            o_ref[...] = y

        @pl.when(j > 0)
        def _():
            o_ref[...] += y

    @pl.when(jnp.logical_and(r >= n_active, j == 0))
    def _():
        o_ref[...] = jnp.zeros_like(o_ref)


def _experts(block_exp, xs, g, wgu, wd):
    R, D = xs.shape
    TM = MOE_TILE
    FF = wd.shape[1]
    TF = FF_TILE_EXPERT
    assert FF % TF == 0 and R % TM == 0
    grid_spec = pltpu.PrefetchScalarGridSpec(
        num_scalar_prefetch=1,
        grid=(R // TM, FF // TF),
        in_specs=[pl.BlockSpec((TM, D), lambda r, j, be: (r, 0)),
                  pl.BlockSpec(g.shape, lambda r, j, be: (0, 0)),
                  pl.BlockSpec((1, D, 2 * TF), lambda r, j, be: (be[r], 0, j)),
                  pl.BlockSpec((1, TF, D), lambda r, j, be: (be[r], j, 0))],
        out_specs=pl.BlockSpec((TM, D), lambda r, j, be: (r, 0)),
        scratch_shapes=[pltpu.VMEM((TM, D), BF16)],
    )
    return pl.pallas_call(
        _expert_kernel,
        grid_spec=grid_spec,
        out_shape=jax.ShapeDtypeStruct((R, D), F32),
        compiler_params=_cparams(("parallel", "arbitrary")),
        name="experts",
    )(block_exp, xs, g, wgu, wd)


def _combine_kernel(dest_ref, h_ref, route_ref, ys_ref, o_ref, buf_ref, sem):
    T = h_ref.shape[0]

    def copy(t, slot):
        return pltpu.make_async_copy(ys_ref.at[pl.ds(dest_ref[0, 0, 2 * t + slot], 1)],
                                     buf_ref.at[slot, pl.ds(t, 1)], sem)

    def issue(t, c):
        copy(t, 0).start()
        copy(t, 1).start()
        return c

    lax.fori_loop(0, T, issue, 0, unroll=8)

    def drain(t, c):
        copy(t, 0).wait()
        copy(t, 1).wait()
        return c

    lax.fori_loop(0, T, drain, 0, unroll=8)
    route = route_ref[...]
    o_ref[...] = h_ref[...] + route[:, 4:5] * buf_ref[0] + route[:, 5:6] * buf_ref[1]


def _combine(dest3, h, route, ys):
    N, D = h.shape
    T = dest3.shape[2] // 2
    return pl.pallas_call(
        _combine_kernel,
        grid=(N // T,),
        in_specs=[pl.BlockSpec((1, 1, 2 * T), lambda i: (i, 0, 0), memory_space=pltpu.SMEM),
                  pl.BlockSpec((T, D), lambda i: (i, 0)),
                  pl.BlockSpec((T, LANE), lambda i: (i, 0)),
                  pl.BlockSpec(memory_space=pl.ANY)],
        out_specs=pl.BlockSpec((T, D), lambda i: (i, 0)),
        out_shape=jax.ShapeDtypeStruct((N, D), F32),
        scratch_shapes=[pltpu.VMEM((2, T, D), F32), pltpu.SemaphoreType.DMA(())],
        compiler_params=_cparams(("arbitrary",)),
        name="combine",
    )(dest3, h, route, ys)


def _moe(mixed, g, wr, wgu, wd):
    N, D = mixed[0].shape
    T = _pick_tile(N, FLAT_TILE)
    h, route, counts = _router(*mixed, g, wr)
    cnt = counts[0, :N_EXPERTS].astype(jnp.int32)
    padded = (cnt + MOE_TILE - 1) // MOE_TILE * MOE_TILE
    pend = jnp.cumsum(padded)
    pstart = pend - padded
    idx = route[:, 0:2].astype(jnp.int32)
    dest = pstart[idx] + route[:, 2:4].astype(jnp.int32)
    dest3 = dest.reshape(N // T, 1, 2 * T)
    n_blocks = -(-2 * N // MOE_TILE) + N_EXPERTS
    block_exp = jnp.minimum(jnp.searchsorted(pend, jnp.arange(n_blocks) * MOE_TILE, side='right'),
                            N_EXPERTS - 1).astype(jnp.int32)
    block_exp = jnp.concatenate([block_exp, (pend[-1:] // MOE_TILE).astype(jnp.int32)])
    seg = jnp.concatenate([pstart + cnt, pend]).astype(jnp.int32)
    xs = _dispatch(seg, dest3, h, n_blocks * MOE_TILE)
    ys = _experts(block_exp, xs, g, wgu, wd)
    return _combine(dest3, h, route, ys)


def _final_kernel(h_ref, g_ref, o_ref):
    o_ref[0] = _rms(h_ref[0, N_META:, :], g_ref[...])


def _final(h, g):
    B, L, D = h.shape
    return pl.pallas_call(
        _final_kernel,
        grid=(B,),
        in_specs=[pl.BlockSpec((1, L, D), lambda b: (b, 0, 0)), pl.BlockSpec(g.shape, lambda b: (0, 0))],
        out_specs=pl.BlockSpec((1, L - N_META, D), lambda b: (b, 0, 0)),
        out_shape=jax.ShapeDtypeStruct((B, L - N_META, D), F32),
        compiler_params=_cparams(("parallel",)),
        name="final_norm",
    )(h, g)


def _rope_tables(L):
    pos = jnp.arange(L, dtype=jnp.int32)
    inv_freq = ROPE_THETA ** (-jnp.arange(0, QK_ROPE, 2, dtype=F32) / QK_ROPE)
    ang = pos.astype(F32)[:, None] * inv_freq[None, :]
    cos, sin = jnp.cos(ang), jnp.sin(ang)
    z = lambda n: jnp.zeros((L, n), F32)
    scale = (QK_NOPE + QK_ROPE) ** -0.5 * float(np.log2(np.e))
    cq = jnp.concatenate([jnp.ones((L, QK_NOPE), F32), cos, cos, z(32)], axis=1) * scale
    s1q = jnp.concatenate([z(QK_NOPE), -sin, z(48)], axis=1) * scale
    s2q = jnp.concatenate([z(QK_NOPE + 16), sin, z(32)], axis=1) * scale
    g = 2 * MLSTM_HEADS
    ck = jnp.concatenate([z(g), cos, cos, z(96 - g)], axis=1)
    s1k = jnp.concatenate([z(g), -sin, z(112 - g)], axis=1)
    s2k = jnp.concatenate([z(g + 16), sin, z(96 - g)], axis=1)
    return jnp.concatenate([cq, s1q, s2q, ck, s1k, s2k], axis=1)


def _interleave_tiles(wg, wu, tile):
    pieces = []
    for c0 in range(0, wg.shape[-1], tile):
        pieces += [wg[..., c0:c0 + tile], wu[..., c0:c0 + tile]]
    return jnp.concatenate(pieces, axis=-1).astype(BF16)


def _layer_weights(l, w_in, w_q_up, w_kv_up, pool_w, w_out):
    pts = np.cumsum([Q_LORA, KV_LORA, QK_ROPE, POOL_WIDTH, 2 * MLSTM_WIDTH, MLSTM_WIDTH, MLSTM_WIDTH]).tolist()
    wi = w_in[l]
    w_cq, w_ckv, w_kr, w_xp, w_mqk, w_mv, w_mo, w_mg = jnp.split(wi, pts, axis=1)
    small = jnp.concatenate([w_mg, w_kr, jnp.zeros((D_MODEL, LANE - QK_ROPE - 2 * MLSTM_HEADS), F32)], axis=1)
    wbig = jnp.concatenate([w_cq, w_ckv, w_xp, w_mqk, w_mv, w_mo, small], axis=1).astype(BF16)
    wq = w_q_up[l].reshape(Q_LORA, MLA_HEADS, QK_NOPE + QK_ROPE)
    wq = jnp.pad(wq, ((0, 0), (0, 0), (0, HEAD_PAD - QK_NOPE - QK_ROPE))).reshape(Q_LORA, -1).astype(BF16)
    wkv = w_kv_up[l].reshape(KV_LORA, MLA_HEADS, QK_NOPE + V_HEAD)
    wk = jnp.pad(wkv[:, :, :QK_NOPE], ((0, 0), (0, 0), (0, HEAD_PAD - QK_NOPE))).reshape(KV_LORA, -1).astype(BF16)
    wv = wkv[:, :, QK_NOPE:].reshape(KV_LORA, -1).astype(BF16)
    pw = jax.scipy.linalg.block_diag(*[pool_w[l, g] for g in range(POOL_GROUPS)]).astype(BF16)
    return dict(wbig=wbig, wq=wq, wk=wk, wv=wv, pw=pw, wo=w_out[l].astype(BF16))


def kernel(x, meta_tokens, attn_norm, w_in, q_norm, w_q_up, kv_norm, w_kv_up, pool_w, pool_scale, conv_w, conv_b, gate_bias, mlstm_norm, w_out, ffn_norm, dense_w_gate, dense_w_up, dense_w_down, router_w, moe_w_gate, moe_w_up, moe_w_down, final_norm):
    B, S, D = x.shape
    L = N_META + S
    N = B * L
    depth = w_in.shape[0]
    n_full = L // SEQ_CHUNK
    assert S % ATTN_TILE == 0 and S % SEQ_CHUNK == 0

    meta = jnp.broadcast_to(meta_tokens[None].astype(x.dtype), (B, N_META, D))
    h = jnp.concatenate([meta, x], axis=1)
    tab = _rope_tables(L)
    row = lambda a: a.reshape(1, -1).astype(F32)

    for l in range(depth):
        lw = _layer_weights(l, w_in, w_q_up, w_kv_up, pool_w, w_out)
        q, k, v, xp, mqk, mv, mo, zs = _inproj(h, row(attn_norm[l]), lw['wbig'], row(q_norm[l]), lw['wq'],
                                               row(kv_norm[l]), lw['wk'], lw['wv'], tab)
        ya = _attention(q, k, v)
        gbc = jnp.pad(gate_bias[l], (0, LANE - 2 * MLSTM_HEADS)).reshape(1, LANE).astype(F32)
        yp, ym = _seq_mixer(xp, mqk, mv, mo, zs, lw['pw'], row(pool_scale[l]),
                            conv_w[l].astype(F32), row(conv_b[l]), gbc, row(mlstm_norm[l]))
        mixed = (h.reshape(N, D), ya.reshape(N, -1), yp.reshape(N, -1), ym.reshape(N, -1),
                 lw['wo'])
        j = l // 2
        if l % 2 == 0:
            wgu = _interleave_tiles(dense_w_gate[j], dense_w_up[j], FF_TILE_DENSE)
            hf = _ffn(*mixed, row(ffn_norm[l]), wgu, dense_w_down[j].astype(BF16))
        else:
            wr = jnp.pad(router_w[j], ((0, 0), (0, LANE - N_EXPERTS))).astype(F32)
            wgu = _interleave_tiles(moe_w_gate[j], moe_w_up[j], FF_TILE_EXPERT)
            hf = _moe(mixed, row(ffn_norm[l]), wr, wgu, moe_w_down[j].astype(BF16))
        h = hf.reshape(B, L, D)
    return _final(h, row(final_norm))
```

```python
import functools

import numpy as np
import jax
import jax.numpy as jnp
from jax import lax
from jax.experimental import pallas as pl
from jax.experimental.pallas import tpu as pltpu

F32 = jnp.float32
BF16 = jnp.bfloat16

D_MODEL = 1024
N_META = 16
CHUNK = 64
MLA_HEADS = 8
QK_NOPE = 64
QK_ROPE = 32
V_HEAD = 64
Q_LORA = 768
KV_LORA = 256
ROPE_THETA = 10000.0
POOL_GROUPS = 4
POOL_GROUP_DIM = 64
POOL_WIDTH = 256
MLSTM_HEADS = 4
MLSTM_HEAD_DIM = 64
MLSTM_WIDTH = 256
CONV_WIDTH = 4
D_FF = 2816
N_EXPERTS = 8
D_FF_EXPERT = 3584
EPS = 1e-6
NEG_INF = -1e30

LANE = 128
HEAD_PAD = 128
ATTN_TILE = 256
SEQ_CHUNK = 256
SEQ_TAIL = 128
SEQ_BATCH = 2
FLAT_TILE = 512
FFN_ROW_TILE = 768
MOE_TILE = 1024
ZERO_ROWS = 256
FF_TILE_DENSE = 1408
FF_TILE_EXPERT = 896
VMEM_LIMIT = 56 * 1024 * 1024

SEG_CQ = (0, 768)
SEG_CKV = (768, 1024)
SEG_XP = (1024, 1280)
SEG_MQK = (1280, 1792)
SEG_MV = (1792, 2048)
SEG_MO = (2048, 2304)
SEG_SMALL = (2304, 2432)
W_BIG = 2432


def _rms(x, g):
    ms = jnp.mean(x * x, axis=-1, keepdims=True)
    return x * lax.rsqrt(ms + EPS) * g


def _dot(a, b):
    return jnp.dot(a, b, preferred_element_type=F32)


def _dot_nt(a, b):
    return lax.dot_general(a, b, (((1,), (1,)), ((), ())), preferred_element_type=F32)


def _dot_tn(a, b):
    return lax.dot_general(a, b, (((0,), (0,)), ((), ())), preferred_element_type=F32)


def _dot_f32(a, b):
    return jnp.dot(a, b, preferred_element_type=F32, precision=lax.Precision.HIGHEST)


def _log_sigmoid(x):
    return jnp.minimum(x, 0.0) - jnp.log(1.0 + jnp.exp(-jnp.abs(x)))


def _pick_tile(n, target):
    best = None
    for t in range(16, target + 1, 16):
        if n % t == 0:
            best = t
    assert best is not None, (n, target)
    return best


def _cparams(sem):
    return pltpu.CompilerParams(dimension_semantics=sem, vmem_limit_bytes=VMEM_LIMIT)


def _rope_lanes(x, c, s_lo, s_hi):
    return x * c + pltpu.roll(x, LANE - 16, 1) * s_lo + pltpu.roll(x, 16, 1) * s_hi


def _inproj_kernel(h_ref, an_ref, wbig_ref, qn_ref, wq_ref, kvn_ref, wk_ref, wv_ref, tab_ref,
                   q_ref, k_ref, v_ref, xp_ref, mqk_ref, mv_ref, mo_ref, zs_ref):
    xn = _rms(h_ref[0], an_ref[...]).astype(BF16)

    z = _dot(xn, wbig_ref[...])

    def seg(ab):
        return z[:, ab[0]:ab[1]]

    xp_ref[0] = seg(SEG_XP).astype(BF16)
    mqk_ref[0] = seg(SEG_MQK).astype(BF16)
    mv_ref[0] = seg(SEG_MV).astype(BF16)
    mo_ref[0] = seg(SEG_MO).astype(BF16)
    zs = seg(SEG_SMALL)
    zs_ref[0] = zs

    tab = tab_ref[...]
    cq, s1q, s2q = tab[:, 0:128], tab[:, 128:256], tab[:, 256:384]
    ck, s1k, s2k = tab[:, 384:512], tab[:, 512:640], tab[:, 640:768]

    cqn = _rms(seg(SEG_CQ), qn_ref[...]).astype(BF16)
    qf = _dot(cqn, wq_ref[...])
    for hd in range(MLA_HEADS):
        sl = slice(HEAD_PAD * hd, HEAD_PAD * (hd + 1))
        q_ref[0, hd] = _rope_lanes(qf[:, sl], cq, s1q, s2q).astype(BF16)

    ckvn = _rms(seg(SEG_CKV), kvn_ref[...]).astype(BF16)
    v_ref[0] = _dot(ckvn, wv_ref[...]).astype(BF16)
    kf = _dot(ckvn, wk_ref[...])
    kr = pltpu.roll(_rope_lanes(zs, ck, s1k, s2k), QK_NOPE - 2 * MLSTM_HEADS, 1)
    for hd in range(MLA_HEADS):
        sl = slice(HEAD_PAD * hd, HEAD_PAD * (hd + 1))
        k_ref[0, hd] = (kf[:, sl] + kr).astype(BF16)


def _inproj(h, an, wbig, qn, wq, kvn, wk, wv, tab):
    B, L, D = h.shape
    T = _pick_tile(L, 768)
    nt = L // T
    row = lambda c: pl.BlockSpec((1, T, c), lambda b, i: (b, i, 0))
    full = lambda a: pl.BlockSpec(a.shape, lambda b, i: (0,) * a.ndim)
    widths = (MLA_HEADS * V_HEAD, POOL_WIDTH, 2 * MLSTM_WIDTH, MLSTM_WIDTH, MLSTM_WIDTH)
    head_major = jax.ShapeDtypeStruct((B, MLA_HEADS, L, HEAD_PAD), BF16)
    head_spec = pl.BlockSpec((1, MLA_HEADS, T, HEAD_PAD), lambda b, i: (b, 0, i, 0))
    out_shape = [head_major, head_major] + [jax.ShapeDtypeStruct((B, L, c), BF16) for c in widths]
    out_shape.append(jax.ShapeDtypeStruct((B, L, LANE), F32))
    out_specs = [head_spec, head_spec] + [row(c) for c in widths] + [row(LANE)]
    return pl.pallas_call(
        _inproj_kernel,
        grid=(B, nt),
        in_specs=[row(D), full(an), full(wbig), full(qn), full(wq), full(kvn), full(wk), full(wv),
                  pl.BlockSpec((T, tab.shape[1]), lambda b, i: (i, 0))],
        out_specs=out_specs,
        out_shape=out_shape,
        compiler_params=_cparams(("parallel", "parallel")),
        name="inproj",
    )(h, an, wbig, qn, wq, kvn, wk, wv, tab)


def _attn_kernel(q_ref, k_ref, v_ref, o_ref, vm_ref, s_ref, p_ref, oacc_ref, *, n_tiles):
    A = ATTN_TILE
    lane = lax.broadcasted_iota(jnp.int32, (1, LANE), 1)
    ri = lax.broadcasted_iota(jnp.int32, (A, A), 0) // CHUNK
    ci = lax.broadcasted_iota(jnp.int32, (A, A), 1) // CHUNK
    diag_mask = ci <= ri
    oacc_ref[...] = jnp.zeros_like(oacc_ref)

    def head_body(hh, carry):
        vm_ref[...] = jnp.where(lane // V_HEAD == hh, v_ref[0], jnp.zeros_like(v_ref[0]))

        s0 = _dot_nt(q_ref[0, hh, 0:N_META, :], k_ref[0, hh, 0:N_META, :])
        p0 = jnp.exp2(s0 - jnp.max(s0, axis=-1, keepdims=True))
        o0 = _dot(p0.astype(BF16), vm_ref[0:N_META, :]) / jnp.sum(p0, axis=-1, keepdims=True)
        oacc_ref[0:N_META, :] += o0

        for j in range(n_tiles):
            r0 = N_META + A * j
            qj = q_ref[0, hh, r0:r0 + A, :]
            s_meta = _dot_nt(qj, k_ref[0, hh, 0:N_META, :])
            m128 = None
            for c in range(j + 1):
                s = _dot_nt(qj, k_ref[0, hh, N_META + A * c:N_META + A * (c + 1), :])
                if c == j:
                    s = jnp.where(diag_mask, s, NEG_INF)
                s_ref[:, A * c:A * (c + 1)] = s
                mc = jnp.maximum(s[:, :LANE], s[:, LANE:])
                m128 = mc if m128 is None else jnp.maximum(m128, mc)
            m = jnp.maximum(jnp.max(m128, axis=-1, keepdims=True),
                            jnp.max(s_meta, axis=-1, keepdims=True))
            p_meta = jnp.exp2(s_meta - m)
            l128 = jnp.zeros((A, LANE), F32)
            for c in range(j + 1):
                p = jnp.exp2(s_ref[:, A * c:A * (c + 1)] - m)
                l128 = l128 + p[:, :LANE] + p[:, LANE:]
                p_ref[:, A * c:A * (c + 1)] = p.astype(BF16)
            l = jnp.sum(p_meta, axis=-1, keepdims=True) + jnp.sum(l128, axis=-1, keepdims=True)
            n = A * (j + 1)
            acc = (_dot(p_ref[:, 0:n], vm_ref[N_META:N_META + n, :])
                   + _dot(p_meta.astype(BF16), vm_ref[0:N_META, :]))
            oacc_ref[r0:r0 + A, :] += acc / l
        return carry

    lax.fori_loop(0, 2, head_body, 0)
    o_ref[0] = oacc_ref[...].astype(o_ref.dtype)


def _attention(q, k, v):
    B, _, L, _ = q.shape
    n_tiles = (L - N_META) // ATTN_TILE
    assert N_META + n_tiles * ATTN_TILE == L
    n_pairs = MLA_HEADS // 2
    qk_spec = pl.BlockSpec((1, 2, L, HEAD_PAD), lambda b, p: (b, p, 0, 0))
    v_spec = pl.BlockSpec((1, L, 2 * V_HEAD), lambda b, p: (b, 0, p))
    return pl.pallas_call(
        functools.partial(_attn_kernel, n_tiles=n_tiles),
        grid=(B, n_pairs),
        in_specs=[qk_spec, qk_spec, v_spec],
        out_specs=v_spec,
        out_shape=jax.ShapeDtypeStruct((B, L, MLA_HEADS * V_HEAD), BF16),
        scratch_shapes=[pltpu.VMEM((L, 2 * V_HEAD), BF16),
                        pltpu.VMEM((ATTN_TILE, n_tiles * ATTN_TILE), F32),
                        pltpu.VMEM((ATTN_TILE, n_tiles * ATTN_TILE), BF16),
                        pltpu.VMEM((L, 2 * V_HEAD), F32)],
        compiler_params=_cparams(("parallel", "parallel")),
        name="attention",
    )(q, k, v)


def _seq_kernel(xp_ref, mqk_ref, mv_ref, mo_ref, zs_ref, pw_ref, ps_ref,
                cw_ref, cb_ref, gbc_ref, mn_ref, yp_ref, ym_ref, ct_ref, m_ref, *, n_full, n_batch):
    HIST = 16
    L = xp_ref.shape[1]
    lane = lax.broadcasted_iota(jnp.int32, (1, LANE), 1)
    lane_lo = lane < MLSTM_HEAD_DIM
    lane256 = lax.broadcasted_iota(jnp.int32, (1, POOL_WIDTH), 1)
    grp = lane256 // POOL_GROUP_DIM
    win = jnp.left_shift(2, grp).astype(F32)
    row8 = lax.broadcasted_iota(jnp.int32, (2 * MLSTM_HEADS, 1), 0)

    ct_ref[...] = jnp.zeros_like(ct_ref)
    m_ref[...] = jnp.zeros_like(m_ref)

    def chunk(T, bb, gcol_raw, xx_p, xx_c, mv_c, mo_c, pos0, store, update_state=True):
        ti = lax.broadcasted_iota(jnp.int32, (T, T), 0)
        si = lax.broadcasted_iota(jnp.int32, (T, T), 1)
        causal = si <= ti
        tril = causal.astype(F32)
        triu = (ti <= si).astype(F32)
        t_col = lax.broadcasted_iota(jnp.int32, (T, 1), 0)
        s2 = xx_p + pltpu.roll(xx_p, 1, 0)
        s4 = s2 + pltpu.roll(s2, 2, 0)
        s8 = s4 + pltpu.roll(s4, 4, 0)
        s16 = s8 + pltpu.roll(s8, 8, 0)
        ssum = jnp.where(grp == 0, s2[HIST:], jnp.where(grp == 1, s4[HIST:],
                         jnp.where(grp == 2, s8[HIST:], s16[HIST:])))
        cnt = jnp.maximum(pos0 + t_col + 1, 1).astype(F32)
        d = ssum / jnp.minimum(cnt, win) - xx_p[HIST:]
        yp = _dot(d.astype(BF16), pw_ref[...]) * ps_ref[...]

        cw = cw_ref[...]
        y = (xx_c * cw[3:4] + pltpu.roll(xx_c, 1, 0) * cw[2:3] + pltpu.roll(xx_c, 2, 0) * cw[1:2]
             + pltpu.roll(xx_c, 3, 0) * cw[0:1])[HIST:] + cb_ref[...]
        qk = y * jax.nn.sigmoid(y)
        qc = qk[:, :MLSTM_WIDTH]
        kc = qk[:, MLSTM_WIDTH:] * (MLSTM_HEAD_DIM ** -0.5)

        gcol = gcol_raw + gbc_ref[...]
        grow = gcol.T[0:2 * MLSTM_HEADS, :]
        is_f = jnp.logical_and(lane >= MLSTM_HEADS, lane < 2 * MLSTM_HEADS)
        b_col = _dot_f32(tril, jnp.where(is_f, _log_sigmoid(gcol), 0.0))
        b_row = _dot_f32(jnp.where(row8 >= MLSTM_HEADS, _log_sigmoid(grow), 0.0), triu)

        ym = []
        for pr in range(MLSTM_HEADS // 2):
            psl = slice(LANE * pr, LANE * (pr + 1))
            q_pair, k_pair = qc[:, psl], kc[:, psl]
            v_pair = mv_c[:, psl]
            h_heads = []
            for e in range(2):
                hd = 2 * pr + e
                own = lane_lo if e == 0 else jnp.logical_not(lane_lo)
                den_lane = MLSTM_HEAD_DIM if e == 0 else 0
                q_h = jnp.where(own, q_pair, 0.0).astype(BF16)
                k_h = jnp.where(own, k_pair, 0.0).astype(BF16)
                v_aug = jnp.where(own, v_pair, jnp.where(lane == den_lane, 1.0, 0.0))
                bc = b_col[:, MLSTM_HEADS + hd:MLSTM_HEADS + hd + 1]
                lic = gcol[:, hd:hd + 1]
                br = b_row[MLSTM_HEADS + hd:MLSTM_HEADS + hd + 1, :]
                lir = grow[hd:hd + 1, :]
                dm = jnp.where(causal, bc + (lir - br), NEG_INF)
                m_prev = m_ref[bb, hd][:, 0:1]
                m_inter = bc + m_prev
                mt = jnp.maximum(m_inter, jnp.max(dm, axis=-1, keepdims=True))
                s = _dot_nt(q_h, k_h) * jnp.exp(dm - mt)
                decay = jnp.exp(m_inter - mt)
                num = (_dot(s.astype(BF16), v_aug.astype(BF16))
                       + decay * _dot(q_h, ct_ref[bb, hd].astype(BF16)))
                den = num[:, den_lane:den_lane + 1]
                h_heads.append(num / jnp.maximum(jnp.abs(den), jnp.exp(-mt)))
                if not update_state:
                    continue
                b_last = bc[T - 1:T, :]
                g_col = b_last - bc + lic
                m_new = jnp.maximum(b_last + m_prev, jnp.max(g_col, axis=0, keepdims=True))
                wg = jnp.exp(g_col - m_new)
                cd = jnp.exp(b_last + m_prev - m_new)
                ct_ref[bb, hd] = cd * ct_ref[bb, hd] + _dot_tn(k_h, (v_aug * wg).astype(BF16))
                m_ref[bb, hd] = jnp.broadcast_to(m_new, (1, LANE))
            h_pair = jnp.where(lane_lo, h_heads[0], h_heads[1])
            sq = h_pair * h_pair
            ms = jnp.where(lane_lo,
                           jnp.sum(jnp.where(lane_lo, sq, 0.0), axis=-1, keepdims=True),
                           jnp.sum(jnp.where(lane_lo, 0.0, sq), axis=-1, keepdims=True)) / MLSTM_HEAD_DIM
            hn = h_pair * lax.rsqrt(ms + EPS) * mn_ref[:, psl]
            ym.append(hn * jax.nn.sigmoid(mo_c[:, psl]))
        store(yp, jnp.concatenate(ym, axis=-1))

    T = SEQ_CHUNK
    TT = SEQ_TAIL
    tail = L - n_full * T

    def store_rows(bb, rows, n=None):
        def store(yp, ym):
            yp_ref[bb, rows, :] = (yp if n is None else yp[:n]).astype(yp_ref.dtype)
            ym_ref[bb, rows, :] = (ym if n is None else ym[:n]).astype(ym_ref.dtype)
        return store

    def head(ref, bb, width):
        return jnp.concatenate([jnp.zeros((HIST, width), F32), ref[bb, 0:T, :].astype(F32)], axis=0)

    for bb in range(n_batch):
        chunk(T, bb, zs_ref[bb, 0:T, :], head(xp_ref, bb, POOL_WIDTH),
              head(mqk_ref, bb, 2 * MLSTM_WIDTH), mv_ref[bb, 0:T, :].astype(F32),
              mo_ref[bb, 0:T, :].astype(F32), 0, store_rows(bb, slice(0, T)))

    def body(c, carry):
        r0 = pl.multiple_of(T * c, T)
        ext = pl.ds(pl.multiple_of(r0 - HIST, 16), T + HIST)
        cur = pl.ds(r0, T)
        for bb in range(n_batch):
            chunk(T, bb, zs_ref[bb, cur, :], xp_ref[bb, ext, :].astype(F32),
                  mqk_ref[bb, ext, :].astype(F32), mv_ref[bb, cur, :].astype(F32),
                  mo_ref[bb, cur, :].astype(F32), r0, store_rows(bb, cur))
        return carry

    lax.fori_loop(1, n_full, body, 0)

    r0 = n_full * T

    def padded(ref, bb, width, lo):
        return jnp.concatenate([ref[bb, lo:L, :].astype(F32), jnp.zeros((TT - tail, width), F32)], axis=0)

    for bb in range(n_batch):
        closed = jnp.broadcast_to(jnp.where(lane < MLSTM_HEADS, NEG_INF, 1e4), (TT - tail, LANE))
        chunk(TT, bb, jnp.concatenate([zs_ref[bb, r0:L, :], closed], axis=0),
              padded(xp_ref, bb, POOL_WIDTH, r0 - HIST),
              padded(mqk_ref, bb, 2 * MLSTM_WIDTH, r0 - HIST), padded(mv_ref, bb, MLSTM_WIDTH, r0),
              padded(mo_ref, bb, MLSTM_WIDTH, r0), r0, store_rows(bb, slice(r0, L), tail),
              update_state=False)


def _seq_mixer(xp, mqk, mv, mo, zs, pw, ps, cw, cb, gbc, mn):
    B, L, _ = xp.shape
    n_full = L // SEQ_CHUNK
    assert 0 < L - n_full * SEQ_CHUNK <= SEQ_TAIL
    BB = SEQ_BATCH
    assert B % BB == 0
    seq = lambda c: pl.BlockSpec((BB, L, c), lambda b: (b, 0, 0))
    full = lambda a: pl.BlockSpec(a.shape, lambda b: (0,) * a.ndim)
    return pl.pallas_call(
        functools.partial(_seq_kernel, n_full=n_full, n_batch=BB),
        grid=(B // BB,),
        in_specs=[seq(POOL_WIDTH), seq(2 * MLSTM_WIDTH), seq(MLSTM_WIDTH), seq(MLSTM_WIDTH),
                  seq(LANE), full(pw), full(ps), full(cw), full(cb), full(gbc), full(mn)],
        out_specs=[seq(POOL_WIDTH), seq(MLSTM_WIDTH)],
        out_shape=[jax.ShapeDtypeStruct((B, L, POOL_WIDTH), BF16),
                   jax.ShapeDtypeStruct((B, L, MLSTM_WIDTH), BF16)],
        scratch_shapes=[pltpu.VMEM((BB, MLSTM_HEADS, LANE, LANE), F32),
                        pltpu.VMEM((BB, MLSTM_HEADS, 1, LANE), F32)],
        compiler_params=_cparams(("parallel",)),
        name="seq_mixer",
    )(xp, mqk, mv, mo, zs, pw, ps, cw, cb, gbc, mn)


def _ffn_kernel(h_ref, ya_ref, yp_ref, ym_ref, wo_ref, g_ref, wgu_ref, wd_ref,
                o_ref, xn_ref, hid_ref):
    hf = h_ref[...] + _dot(jnp.concatenate([ya_ref[...], yp_ref[...], ym_ref[...]], axis=1), wo_ref[...])
    o_ref[...] = hf
    xn_ref[...] = _rms(hf, g_ref[...]).astype(BF16)
    ff = hid_ref.shape[1]
    ft = FF_TILE_DENSE
    for c0 in range(0, ff, ft):
        au = _dot(xn_ref[...], wgu_ref[:, 2 * c0:2 * (c0 + ft)])
        a, u = au[:, :ft], au[:, ft:]
        hid_ref[:, c0:c0 + ft] = (a * jax.nn.sigmoid(a) * u).astype(BF16)
    o_ref[...] += _dot(hid_ref[...], wd_ref[...])


def _ffn(h, ya, yp, ym, wo, g, wgu, wd):
    N, D = h.shape
    FF = wd.shape[0]
    T = _pick_tile(N, FFN_ROW_TILE)
    assert FF % FF_TILE_DENSE == 0
    row = lambda a: pl.BlockSpec((T, a.shape[1]), lambda i: (i, 0))
    full = lambda a: pl.BlockSpec(a.shape, lambda i: (0, 0), pipeline_mode=pl.Buffered(1))
    return pl.pallas_call(
        _ffn_kernel,
        grid=(N // T,),
        in_specs=[row(h), row(ya), row(yp), row(ym), full(wo), full(g),
                  full(wgu), full(wd)],
        out_specs=row(h),
        out_shape=jax.ShapeDtypeStruct((N, D), F32),
        scratch_shapes=[pltpu.VMEM((T, D), BF16), pltpu.VMEM((T, FF), BF16)],
        compiler_params=_cparams(("parallel",)),
        name="dense_ffn",
    )(h, ya, yp, ym, wo, g, wgu, wd)


def _router_kernel(h_ref, ya_ref, yp_ref, ym_ref, wo_ref, g_ref, wrh_ref, wrl_ref, tri_ref,
                   hf_ref, route_ref, cnt_ref, run_ref):
    i = pl.program_id(0)
    T = h_ref.shape[0]

    @pl.when(i == 0)
    def _():
        run_ref[...] = jnp.zeros_like(run_ref)

    hf = h_ref[...] + _dot(jnp.concatenate([ya_ref[...], yp_ref[...], ym_ref[...]], axis=1), wo_ref[...])
    hf_ref[...] = hf
    xn = _rms(hf, g_ref[...])
    lane = lax.broadcasted_iota(jnp.int32, (T, LANE), 1).astype(F32)
    x_hi = xn.astype(BF16)
    x_lo = (xn - x_hi.astype(F32)).astype(BF16)
    logits = _dot(x_hi, wrh_ref[...]) + (_dot(x_lo, wrh_ref[...]) + _dot(x_hi, wrl_ref[...]))
    logits = jnp.where(lane < N_EXPERTS, logits, NEG_INF)
    v1 = jnp.max(logits, axis=-1, keepdims=True)
    i1 = jnp.min(jnp.where(logits == v1, lane, float(LANE)), axis=-1, keepdims=True)
    rest = jnp.where(lane == i1, NEG_INF, logits)
    v2 = jnp.max(rest, axis=-1, keepdims=True)
    i2 = jnp.min(jnp.where(rest == v2, lane, float(LANE)), axis=-1, keepdims=True)
    e2 = jnp.exp(v2 - v1)
    g1 = 1.0 / (1.0 + e2)
    g2 = e2 / (1.0 + e2)
    member = jnp.logical_or(lane == i1, lane == i2).astype(F32)
    rank = run_ref[0:1, :] + _dot(tri_ref[...], member.astype(BF16))
    r1 = jnp.sum(jnp.where(lane == i1, rank, 0.0), axis=-1, keepdims=True)
    r2 = jnp.sum(jnp.where(lane == i2, rank, 0.0), axis=-1, keepdims=True)
    run_ref[...] = run_ref[...] + jnp.sum(member, axis=0, keepdims=True)
    cnt_ref[...] = run_ref[...]
    route = jnp.where(lane == 0, i1, jnp.where(lane == 1, i2,
            jnp.where(lane == 2, r1, jnp.where(lane == 3, r2,
            jnp.where(lane == 4, g1, jnp.where(lane == 5, g2, 0.0))))))
    route_ref[...] = route


def _router(h, ya, yp, ym, wo, g, wr):
    N, D = h.shape
    T = _pick_tile(N, FLAT_TILE)
    tri = jnp.tril(jnp.ones((T, T), BF16), -1)
    wr_hi = wr.astype(BF16)
    wr_lo = (wr - wr_hi.astype(F32)).astype(BF16)
    row = lambda a: pl.BlockSpec((T, a.shape[1]), lambda i: (i, 0))
    full = lambda a: pl.BlockSpec(a.shape, lambda i: (0, 0))
    return pl.pallas_call(
        _router_kernel,
        grid=(N // T,),
        in_specs=[row(h), row(ya), row(yp), row(ym), full(wo), full(g), full(wr_hi),
                  full(wr_lo), full(tri)],
        out_specs=[pl.BlockSpec((T, D), lambda i: (i, 0)),
                   pl.BlockSpec((T, LANE), lambda i: (i, 0)),
                   pl.BlockSpec((8, LANE), lambda i: (0, 0))],
        out_shape=[jax.ShapeDtypeStruct((N, D), F32), jax.ShapeDtypeStruct((N, LANE), F32),
                   jax.ShapeDtypeStruct((8, LANE), F32)],
        scratch_shapes=[pltpu.VMEM((8, LANE), F32)],
        compiler_params=_cparams(("arbitrary",)),
        name="router",
    )(h, ya, yp, ym, wo, g, wr_hi, wr_lo, tri)


def _dispatch_kernel(seg_ref, dest_ref, h_ref, xs_ref, zero_ref, sem):
    n = h_ref.shape[0]
    zb = zero_ref.shape[0]
    n_rows = xs_ref.shape[0]

    def pad_row(r):
        return pltpu.make_async_copy(zero_ref.at[pl.ds(0, 1)], xs_ref.at[pl.ds(r, 1)], sem)

    def tail_block(b):
        start = pl.multiple_of(seg_ref[2 * N_EXPERTS - 1] + b * zb, zb)
        return pltpu.make_async_copy(zero_ref, xs_ref.at[pl.ds(start, zb)], sem)

    def for_unwritten(start):
        for e in range(N_EXPERTS):
            lax.fori_loop(seg_ref[e], seg_ref[N_EXPERTS + e],
                          lambda r, c: (pad_row(r).start() if start else pad_row(r).wait(), c)[1], 0)
        lax.fori_loop(0, (n_rows - seg_ref[2 * N_EXPERTS - 1]) // zb,
                      lambda b, c: (tail_block(b).start() if start else tail_block(b).wait(), c)[1], 0)

    @pl.when(pl.program_id(0) == 0)
    def _():
        zero_ref[...] = jnp.zeros_like(zero_ref)
        for_unwritten(True)
        for_unwritten(False)

    def copy(t, slot):
        return pltpu.make_async_copy(h_ref.at[pl.ds(t, 1)],
                                     xs_ref.at[pl.ds(dest_ref[0, 0, 2 * t + slot], 1)], sem)

    def issue(t, c):
        copy(t, 0).start()
        copy(t, 1).start()
        return c

    lax.fori_loop(0, n, issue, 0, unroll=8)

    def drain(t, c):
        copy(t, 0).wait()
        copy(t, 1).wait()
        return c

    lax.fori_loop(0, n, drain, 0, unroll=8)


def _dispatch(seg, dest3, h, n_rows):
    n_tiles = dest3.shape[0]
    T, D = dest3.shape[2] // 2, h.shape[1]
    grid_spec = pltpu.PrefetchScalarGridSpec(
        num_scalar_prefetch=1,
        grid=(n_tiles,),
        in_specs=[pl.BlockSpec((1, 1, 2 * T), lambda i, seg: (i, 0, 0), memory_space=pltpu.SMEM),
                  pl.BlockSpec((T, D), lambda i, seg: (i, 0))],
        out_specs=pl.BlockSpec(memory_space=pl.ANY),
        scratch_shapes=[pltpu.VMEM((ZERO_ROWS, D), F32), pltpu.SemaphoreType.DMA(())],
    )
    return pl.pallas_call(
        _dispatch_kernel,
        grid_spec=grid_spec,
        out_shape=jax.ShapeDtypeStruct((n_rows, D), F32),
        compiler_params=_cparams(("arbitrary",)),
        name="dispatch",
    )(seg, dest3, h)


def _expert_kernel(be_ref, x_ref, g_ref, wgu_ref, wd_ref, o_ref, xn_ref):
    r = pl.program_id(0)
    j = pl.program_id(1)
    n_active = be_ref[be_ref.shape[0] - 1]

    @pl.when(r < n_active)
    def _():
        @pl.when(j == 0)
        def _():
            xn_ref[...] = _rms(x_ref[...], g_ref[...]).astype(BF16)

        tf = wd_ref.shape[1]
        au = _dot(xn_ref[...], wgu_ref[0])
        a, u = au[:, :tf], au[:, tf:]
        y = _dot((a * jax.nn.sigmoid(a) * u).astype(BF16), wd_ref[0])

        @pl.when(j == 0)
        def _():
            o_ref[...] = y

        @pl.when(j > 0)
        def _():
            o_ref[...] += y

    @pl.when(jnp.logical_and(r >= n_active, j == 0))
    def _():
        o_ref[...] = jnp.zeros_like(o_ref)


def _experts(block_exp, xs, g, wgu, wd):
    R, D = xs.shape
    TM = MOE_TILE
    FF = wd.shape[1]
    TF = FF_TILE_EXPERT
    assert FF % TF == 0 and R % TM == 0
    grid_spec = pltpu.PrefetchScalarGridSpec(
        num_scalar_prefetch=1,
        grid=(R // TM, FF // TF),
        in_specs=[pl.BlockSpec((TM, D), lambda r, j, be: (r, 0)),
                  pl.BlockSpec(g.shape, lambda r, j, be: (0, 0)),
                  pl.BlockSpec((1, D, 2 * TF), lambda r, j, be: (be[r], 0, j)),
                  pl.BlockSpec((1, TF, D), lambda r, j, be: (be[r], j, 0))],
        out_specs=pl.BlockSpec((TM, D), lambda r, j, be: (r, 0)),
        scratch_shapes=[pltpu.VMEM((TM, D), BF16)],
    )
    return pl.pallas_call(
        _expert_kernel,
        grid_spec=grid_spec,
        out_shape=jax.ShapeDtypeStruct((R, D), F32),
        compiler_params=_cparams(("parallel", "arbitrary")),
        name="experts",
    )(block_exp, xs, g, wgu, wd)


def _combine_kernel(dest_ref, next_ref, h_ref, route_ref, ys_ref, o_ref, buf_ref, sems):
    i = pl.program_id(0)
    T = h_ref.shape[0]
    cur = i % 2

    def copy(idx_ref, b, t, slot):
        return pltpu.make_async_copy(ys_ref.at[pl.ds(idx_ref[0, 0, 2 * t + slot], 1)],
                                     buf_ref.at[b, slot, pl.ds(t, 1)], sems.at[b])

    def issue(idx_ref, b):
        def body(t, c):
            copy(idx_ref, b, t, 0).start()
            copy(idx_ref, b, t, 1).start()
            return c
        lax.fori_loop(0, T, body, 0, unroll=8)

    @pl.when(i == 0)
    def _():
        issue(dest_ref, 0)

    @pl.when(i + 1 < pl.num_programs(0))
    def _():
        issue(next_ref, 1 - cur)

    def drain(t, c):
        copy(dest_ref, cur, t, 0).wait()
        copy(dest_ref, cur, t, 1).wait()
        return c

    lax.fori_loop(0, T, drain, 0, unroll=8)
    route = route_ref[...]
    o_ref[...] = h_ref[...] + route[:, 4:5] * buf_ref[cur, 0] + route[:, 5:6] * buf_ref[cur, 1]


def _combine(dest3, h, route, ys):
    N, D = h.shape
    T = dest3.shape[2] // 2
    n_tiles = N // T
    idx_spec = lambda f: pl.BlockSpec((1, 1, 2 * T), lambda i: (f(i), 0, 0), memory_space=pltpu.SMEM)
    return pl.pallas_call(
        _combine_kernel,
        grid=(n_tiles,),
        in_specs=[idx_spec(lambda i: i), idx_spec(lambda i: jnp.minimum(i + 1, n_tiles - 1)),
                  pl.BlockSpec((T, D), lambda i: (i, 0)),
                  pl.BlockSpec((T, LANE), lambda i: (i, 0)),
                  pl.BlockSpec(memory_space=pl.ANY)],
        out_specs=pl.BlockSpec((T, D), lambda i: (i, 0)),
        out_shape=jax.ShapeDtypeStruct((N, D), F32),
        scratch_shapes=[pltpu.VMEM((2, 2, T, D), F32), pltpu.SemaphoreType.DMA((2,))],
        compiler_params=_cparams(("arbitrary",)),
        name="combine",
    )(dest3, dest3, h, route, ys)


def _moe(mixed, g, wr, wgu, wd):
    N, D = mixed[0].shape
    T = _pick_tile(N, FLAT_TILE)
    h, route, counts = _router(*mixed, g, wr)
    cnt = counts[0, :N_EXPERTS].astype(jnp.int32)
    padded = (cnt + MOE_TILE - 1) // MOE_TILE * MOE_TILE
    pend = jnp.cumsum(padded)
    pstart = pend - padded
    idx = route[:, 0:2].astype(jnp.int32)
    dest = pstart[idx] + route[:, 2:4].astype(jnp.int32)
    dest3 = dest.reshape(N // T, 1, 2 * T)
    n_blocks = -(-2 * N // MOE_TILE) + N_EXPERTS
    block_exp = jnp.minimum(jnp.searchsorted(pend, jnp.arange(n_blocks) * MOE_TILE, side='right'),
                            N_EXPERTS - 1).astype(jnp.int32)
    block_exp = jnp.concatenate([block_exp, (pend[-1:] // MOE_TILE).astype(jnp.int32)])
    seg = jnp.concatenate([pstart + cnt, pend]).astype(jnp.int32)
    xs = _dispatch(seg, dest3, h, n_blocks * MOE_TILE)
    ys = _experts(block_exp, xs, g, wgu, wd)
    return _combine(dest3, h, route, ys)


def _final_kernel(h_ref, g_ref, o_ref):
    o_ref[0] = _rms(h_ref[0, N_META:, :], g_ref[...])


def _final(h, g):
    B, L, D = h.shape
    return pl.pallas_call(
        _final_kernel,
        grid=(B,),
        in_specs=[pl.BlockSpec((1, L, D), lambda b: (b, 0, 0)), pl.BlockSpec(g.shape, lambda b: (0, 0))],
        out_specs=pl.BlockSpec((1, L - N_META, D), lambda b: (b, 0, 0)),
        out_shape=jax.ShapeDtypeStruct((B, L - N_META, D), F32),
        compiler_params=_cparams(("parallel",)),
        name="final_norm",
    )(h, g)


def _rope_tables(L):
    pos = jnp.arange(L, dtype=jnp.int32)
    inv_freq = ROPE_THETA ** (-jnp.arange(0, QK_ROPE, 2, dtype=F32) / QK_ROPE)
    ang = pos.astype(F32)[:, None] * inv_freq[None, :]
    cos, sin = jnp.cos(ang), jnp.sin(ang)
    z = lambda n: jnp.zeros((L, n), F32)
    scale = (QK_NOPE + QK_ROPE) ** -0.5 * float(np.log2(np.e))
    cq = jnp.concatenate([jnp.ones((L, QK_NOPE), F32), cos, cos, z(32)], axis=1) * scale
    s1q = jnp.concatenate([z(QK_NOPE), -sin, z(48)], axis=1) * scale
    s2q = jnp.concatenate([z(QK_NOPE + 16), sin, z(32)], axis=1) * scale
    g = 2 * MLSTM_HEADS
    ck = jnp.concatenate([z(g), cos, cos, z(96 - g)], axis=1)
    s1k = jnp.concatenate([z(g), -sin, z(112 - g)], axis=1)
    s2k = jnp.concatenate([z(g + 16), sin, z(96 - g)], axis=1)
    return jnp.concatenate([cq, s1q, s2q, ck, s1k, s2k], axis=1)


def _interleave_tiles(wg, wu, tile):
    pieces = []
    for c0 in range(0, wg.shape[-1], tile):
        pieces += [wg[..., c0:c0 + tile], wu[..., c0:c0 + tile]]
    return jnp.concatenate(pieces, axis=-1).astype(BF16)


def _layer_weights(l, w_in, w_q_up, w_kv_up, pool_w, w_out):
    pts = np.cumsum([Q_LORA, KV_LORA, QK_ROPE, POOL_WIDTH, 2 * MLSTM_WIDTH, MLSTM_WIDTH, MLSTM_WIDTH]).tolist()
    wi = w_in[l]
    w_cq, w_ckv, w_kr, w_xp, w_mqk, w_mv, w_mo, w_mg = jnp.split(wi, pts, axis=1)
    small = jnp.concatenate([w_mg, w_kr, jnp.zeros((D_MODEL, LANE - QK_ROPE - 2 * MLSTM_HEADS), F32)], axis=1)
    wbig = jnp.concatenate([w_cq, w_ckv, w_xp, w_mqk, w_mv, w_mo, small], axis=1).astype(BF16)
    wq = w_q_up[l].reshape(Q_LORA, MLA_HEADS, QK_NOPE + QK_ROPE)
    wq = jnp.pad(wq, ((0, 0), (0, 0), (0, HEAD_PAD - QK_NOPE - QK_ROPE))).reshape(Q_LORA, -1).astype(BF16)
    wkv = w_kv_up[l].reshape(KV_LORA, MLA_HEADS, QK_NOPE + V_HEAD)
    wk = jnp.pad(wkv[:, :, :QK_NOPE], ((0, 0), (0, 0), (0, HEAD_PAD - QK_NOPE))).reshape(KV_LORA, -1).astype(BF16)
    wv = wkv[:, :, QK_NOPE:].reshape(KV_LORA, -1).astype(BF16)
    pw = jax.scipy.linalg.block_diag(*[pool_w[l, g] for g in range(POOL_GROUPS)]).astype(BF16)
    return dict(wbig=wbig, wq=wq, wk=wk, wv=wv, pw=pw, wo=w_out[l].astype(BF16))


def kernel(x, meta_tokens, attn_norm, w_in, q_norm, w_q_up, kv_norm, w_kv_up, pool_w, pool_scale, conv_w, conv_b, gate_bias, mlstm_norm, w_out, ffn_norm, dense_w_gate, dense_w_up, dense_w_down, router_w, moe_w_gate, moe_w_up, moe_w_down, final_norm):
    B, S, D = x.shape
    L = N_META + S
    N = B * L
    depth = w_in.shape[0]
    n_full = L // SEQ_CHUNK
    assert S % ATTN_TILE == 0 and S % SEQ_CHUNK == 0

    meta = jnp.broadcast_to(meta_tokens[None].astype(x.dtype), (B, N_META, D))
    h = jnp.concatenate([meta, x], axis=1)
    tab = _rope_tables(L)
    row = lambda a: a.reshape(1, -1).astype(F32)

    for l in range(depth):
        lw = _layer_weights(l, w_in, w_q_up, w_kv_up, pool_w, w_out)
        q, k, v, xp, mqk, mv, mo, zs = _inproj(h, row(attn_norm[l]), lw['wbig'], row(q_norm[l]), lw['wq'],
                                               row(kv_norm[l]), lw['wk'], lw['wv'], tab)
        ya = _attention(q, k, v)
        gbc = jnp.pad(gate_bias[l], (0, LANE - 2 * MLSTM_HEADS)).reshape(1, LANE).astype(F32)
        yp, ym = _seq_mixer(xp, mqk, mv, mo, zs, lw['pw'], row(pool_scale[l]),
                            conv_w[l].astype(F32), row(conv_b[l]), gbc, row(mlstm_norm[l]))
        mixed = (h.reshape(N, D), ya.reshape(N, -1), yp.reshape(N, -1), ym.reshape(N, -1),
                 lw['wo'])
        j = l // 2
        if l % 2 == 0:
            wgu = _interleave_tiles(dense_w_gate[j], dense_w_up[j], FF_TILE_DENSE)
            hf = _ffn(*mixed, row(ffn_norm[l]), wgu, dense_w_down[j].astype(BF16))
        else:
            wr = jnp.pad(router_w[j], ((0, 0), (0, LANE - N_EXPERTS))).astype(F32)
            wgu = _interleave_tiles(moe_w_gate[j], moe_w_up[j], FF_TILE_EXPERT)
            hf = _moe(mixed, row(ffn_norm[l]), wr, wgu, moe_w_down[j].astype(BF16))
        h = hf.reshape(B, L, D)
    return _final(h, row(final_norm))
```

```python
import functools

import numpy as np
import jax
import jax.numpy as jnp
from jax import lax
from jax.experimental import pallas as pl
from jax.experimental.pallas import tpu as pltpu

F32 = jnp.float32
BF16 = jnp.bfloat16

D_MODEL = 1024
N_META = 16
CHUNK = 64
MLA_HEADS = 8
QK_NOPE = 64
QK_ROPE = 32
V_HEAD = 64
Q_LORA = 768
KV_LORA = 256
ROPE_THETA = 10000.0
POOL_GROUPS = 4
POOL_GROUP_DIM = 64
POOL_WIDTH = 256
MLSTM_HEADS = 4
MLSTM_HEAD_DIM = 64
MLSTM_WIDTH = 256
CONV_WIDTH = 4
D_FF = 2816
N_EXPERTS = 8
D_FF_EXPERT = 3584
EPS = 1e-6
NEG_INF = -1e30

LANE = 128
HEAD_PAD = 128
ATTN_TILE = 256
SEQ_CHUNK = 256
SEQ_TAIL = 128
SEQ_BATCH = 2
FLAT_TILE = 512
FFN_ROW_TILE = 768
MOE_TILE = 1024
ZERO_ROWS = 256
FF_TILE_DENSE = 1408
FF_TILE_EXPERT = 896
VMEM_LIMIT = 56 * 1024 * 1024

SEG_CQ = (0, 768)
SEG_CKV = (768, 1024)
SEG_XP = (1024, 1280)
SEG_MQK = (1280, 1792)
SEG_MV = (1792, 2048)
SEG_MO = (2048, 2304)
SEG_SMALL = (2304, 2432)
W_BIG = 2432


def _rms(x, g):
    ms = jnp.mean(x * x, axis=-1, keepdims=True)
    return x * lax.rsqrt(ms + EPS) * g


def _dot(a, b):
    return jnp.dot(a, b, preferred_element_type=F32)


def _dot_nt(a, b):
    return lax.dot_general(a, b, (((1,), (1,)), ((), ())), preferred_element_type=F32)


def _dot_tn(a, b):
    return lax.dot_general(a, b, (((0,), (0,)), ((), ())), preferred_element_type=F32)


def _dot_f32(a, b):
    return jnp.dot(a, b, preferred_element_type=F32, precision=lax.Precision.HIGHEST)


def _log_sigmoid(x):
    return jnp.minimum(x, 0.0) - jnp.log(1.0 + jnp.exp(-jnp.abs(x)))


def _pick_tile(n, target):
    best = None
    for t in range(16, target + 1, 16):
        if n % t == 0:
            best = t
    assert best is not None, (n, target)
    return best


def _cparams(sem):
    return pltpu.CompilerParams(dimension_semantics=sem, vmem_limit_bytes=VMEM_LIMIT)


def _rope_lanes(x, c, s_lo, s_hi):
    return x * c + pltpu.roll(x, LANE - 16, 1) * s_lo + pltpu.roll(x, 16, 1) * s_hi


def _inproj_kernel(h_ref, an_ref, wbig_ref, qn_ref, wq_ref, kvn_ref, wk_ref, wv_ref, tab_ref,
                   q_ref, k_ref, v_ref, xp_ref, mqk_ref, mv_ref, mo_ref, zs_ref):
    xn = _rms(h_ref[0], an_ref[...]).astype(BF16)

    z = _dot(xn, wbig_ref[...])

    def seg(ab):
        return z[:, ab[0]:ab[1]]

    xp_ref[0] = seg(SEG_XP).astype(BF16)
    mqk_ref[0] = seg(SEG_MQK).astype(BF16)
    mv_ref[0] = seg(SEG_MV).astype(BF16)
    mo_ref[0] = seg(SEG_MO).astype(BF16)
    zs = seg(SEG_SMALL)
    zs_ref[0] = zs

    tab = tab_ref[...]
    cq, s1q, s2q = tab[:, 0:128], tab[:, 128:256], tab[:, 256:384]
    ck, s1k, s2k = tab[:, 384:512], tab[:, 512:640], tab[:, 640:768]

    cqn = _rms(seg(SEG_CQ), qn_ref[...]).astype(BF16)
    qf = _dot(cqn, wq_ref[...])
    for hd in range(MLA_HEADS):
        sl = slice(HEAD_PAD * hd, HEAD_PAD * (hd + 1))
        q_ref[0, hd] = _rope_lanes(qf[:, sl], cq, s1q, s2q).astype(BF16)

    ckvn = _rms(seg(SEG_CKV), kvn_ref[...]).astype(BF16)
    v_ref[0] = _dot(ckvn, wv_ref[...]).astype(BF16)
    kf = _dot(ckvn, wk_ref[...])
    kr = pltpu.roll(_rope_lanes(zs, ck, s1k, s2k), QK_NOPE - 2 * MLSTM_HEADS, 1)
    for hd in range(MLA_HEADS):
        sl = slice(HEAD_PAD * hd, HEAD_PAD * (hd + 1))
        k_ref[0, hd] = (kf[:, sl] + kr).astype(BF16)


def _inproj(h, an, wbig, qn, wq, kvn, wk, wv, tab):
    B, L, D = h.shape
    T = _pick_tile(L, 768)
    nt = L // T
    row = lambda c: pl.BlockSpec((1, T, c), lambda b, i: (b, i, 0))
    full = lambda a: pl.BlockSpec(a.shape, lambda b, i: (0,) * a.ndim)
    widths = (MLA_HEADS * V_HEAD, POOL_WIDTH, 2 * MLSTM_WIDTH, MLSTM_WIDTH, MLSTM_WIDTH)
    head_major = jax.ShapeDtypeStruct((B, MLA_HEADS, L, HEAD_PAD), BF16)
    head_spec = pl.BlockSpec((1, MLA_HEADS, T, HEAD_PAD), lambda b, i: (b, 0, i, 0))
    out_shape = [head_major, head_major] + [jax.ShapeDtypeStruct((B, L, c), BF16) for c in widths]
    out_shape.append(jax.ShapeDtypeStruct((B, L, LANE), F32))
    out_specs = [head_spec, head_spec] + [row(c) for c in widths] + [row(LANE)]
    return pl.pallas_call(
        _inproj_kernel,
        grid=(B, nt),
        in_specs=[row(D), full(an), full(wbig), full(qn), full(wq), full(kvn), full(wk), full(wv),
                  pl.BlockSpec((T, tab.shape[1]), lambda b, i: (i, 0))],
        out_specs=out_specs,
        out_shape=out_shape,
        compiler_params=_cparams(("parallel", "parallel")),
        name="inproj",
    )(h, an, wbig, qn, wq, kvn, wk, wv, tab)


def _attn_kernel(q_ref, k_ref, v_ref, o_ref, vm_ref, s_ref, p_ref, oacc_ref, *, n_tiles):
    A = ATTN_TILE
    lane = lax.broadcasted_iota(jnp.int32, (1, LANE), 1)
    ri = lax.broadcasted_iota(jnp.int32, (A, A), 0) // CHUNK
    ci = lax.broadcasted_iota(jnp.int32, (A, A), 1) // CHUNK
    diag_mask = ci <= ri
    oacc_ref[...] = jnp.zeros_like(oacc_ref)

    def head_body(hh, carry):
        vm_ref[...] = jnp.where(lane // V_HEAD == hh, v_ref[0], jnp.zeros_like(v_ref[0]))

        s0 = _dot_nt(q_ref[0, hh, 0:N_META, :], k_ref[0, hh, 0:N_META, :])
        p0 = jnp.exp2(s0 - jnp.max(s0, axis=-1, keepdims=True))
        o0 = _dot(p0.astype(BF16), vm_ref[0:N_META, :]) / jnp.sum(p0, axis=-1, keepdims=True)
        oacc_ref[0:N_META, :] += o0

        for j in range(n_tiles):
            r0 = N_META + A * j
            qj = q_ref[0, hh, r0:r0 + A, :]
            s_meta = _dot_nt(qj, k_ref[0, hh, 0:N_META, :])
            m128 = None
            for c in range(j + 1):
                s = _dot_nt(qj, k_ref[0, hh, N_META + A * c:N_META + A * (c + 1), :])
                if c == j:
                    s = jnp.where(diag_mask, s, NEG_INF)
                s_ref[:, A * c:A * (c + 1)] = s
                mc = jnp.maximum(s[:, :LANE], s[:, LANE:])
                m128 = mc if m128 is None else jnp.maximum(m128, mc)
            m = jnp.maximum(jnp.max(m128, axis=-1, keepdims=True),
                            jnp.max(s_meta, axis=-1, keepdims=True))
            p_meta = jnp.exp2(s_meta - m)
            l128 = jnp.zeros((A, LANE), F32)
            for c in range(j + 1):
                p = jnp.exp2(s_ref[:, A * c:A * (c + 1)] - m)
                l128 = l128 + p[:, :LANE] + p[:, LANE:]
                p_ref[:, A * c:A * (c + 1)] = p.astype(BF16)
            l = jnp.sum(p_meta, axis=-1, keepdims=True) + jnp.sum(l128, axis=-1, keepdims=True)
            n = A * (j + 1)
            acc = (_dot(p_ref[:, 0:n], vm_ref[N_META:N_META + n, :])
                   + _dot(p_meta.astype(BF16), vm_ref[0:N_META, :]))
            oacc_ref[r0:r0 + A, :] += acc / l
        return carry

    lax.fori_loop(0, 2, head_body, 0)
    o_ref[0] = oacc_ref[...].astype(o_ref.dtype)


def _attention(q, k, v):
    B, _, L, _ = q.shape
    n_tiles = (L - N_META) // ATTN_TILE
    assert N_META + n_tiles * ATTN_TILE == L
    n_pairs = MLA_HEADS // 2
    qk_spec = pl.BlockSpec((1, 2, L, HEAD_PAD), lambda b, p: (b, p, 0, 0))
    v_spec = pl.BlockSpec((1, L, 2 * V_HEAD), lambda b, p: (b, 0, p))
    return pl.pallas_call(
        functools.partial(_attn_kernel, n_tiles=n_tiles),
        grid=(B, n_pairs),
        in_specs=[qk_spec, qk_spec, v_spec],
        out_specs=v_spec,
        out_shape=jax.ShapeDtypeStruct((B, L, MLA_HEADS * V_HEAD), BF16),
        scratch_shapes=[pltpu.VMEM((L, 2 * V_HEAD), BF16),
                        pltpu.VMEM((ATTN_TILE, n_tiles * ATTN_TILE), F32),
                        pltpu.VMEM((ATTN_TILE, n_tiles * ATTN_TILE), BF16),
                        pltpu.VMEM((L, 2 * V_HEAD), F32)],
        compiler_params=_cparams(("parallel", "parallel")),
        name="attention",
    )(q, k, v)


def _seq_kernel(xp_ref, mqk_ref, mv_ref, mo_ref, zs_ref, pw_ref, ps_ref,
                cw_ref, cb_ref, gbc_ref, mn_ref, yp_ref, ym_ref, ct_ref, m_ref, *, n_full, n_batch):
    HIST = 16
    L = xp_ref.shape[1]
    lane = lax.broadcasted_iota(jnp.int32, (1, LANE), 1)
    lane_lo = lane < MLSTM_HEAD_DIM
    lane256 = lax.broadcasted_iota(jnp.int32, (1, POOL_WIDTH), 1)
    grp = lane256 // POOL_GROUP_DIM
    win = jnp.left_shift(2, grp).astype(F32)
    row8 = lax.broadcasted_iota(jnp.int32, (2 * MLSTM_HEADS, 1), 0)

    ct_ref[...] = jnp.zeros_like(ct_ref)
    m_ref[...] = jnp.zeros_like(m_ref)

    def chunk(T, bb, gcol_raw, xx_p, xx_c, mv_c, mo_c, pos0, store, update_state=True):
        ti = lax.broadcasted_iota(jnp.int32, (T, T), 0)
        si = lax.broadcasted_iota(jnp.int32, (T, T), 1)
        causal = si <= ti
        tril = causal.astype(F32)
        triu = (ti <= si).astype(F32)
        t_col = lax.broadcasted_iota(jnp.int32, (T, 1), 0)
        s2 = xx_p + pltpu.roll(xx_p, 1, 0)
        s4 = s2 + pltpu.roll(s2, 2, 0)
        s8 = s4 + pltpu.roll(s4, 4, 0)
        s16 = s8 + pltpu.roll(s8, 8, 0)
        ssum = jnp.where(grp == 0, s2[HIST:], jnp.where(grp == 1, s4[HIST:],
                         jnp.where(grp == 2, s8[HIST:], s16[HIST:])))
        cnt = jnp.maximum(pos0 + t_col + 1, 1).astype(F32)
        d = ssum / jnp.minimum(cnt, win) - xx_p[HIST:]
        yp = _dot(d.astype(BF16), pw_ref[...]) * ps_ref[...]

        cw = cw_ref[...]
        y = (xx_c * cw[3:4] + pltpu.roll(xx_c, 1, 0) * cw[2:3] + pltpu.roll(xx_c, 2, 0) * cw[1:2]
             + pltpu.roll(xx_c, 3, 0) * cw[0:1])[HIST:] + cb_ref[...]
        qk = y * jax.nn.sigmoid(y)
        qc = qk[:, :MLSTM_WIDTH]
        kc = qk[:, MLSTM_WIDTH:] * (MLSTM_HEAD_DIM ** -0.5)

        gcol = gcol_raw + gbc_ref[...]
        grow = gcol.T[0:2 * MLSTM_HEADS, :]
        is_f = jnp.logical_and(lane >= MLSTM_HEADS, lane < 2 * MLSTM_HEADS)
        b_col = _dot_f32(tril, jnp.where(is_f, _log_sigmoid(gcol), 0.0))
        b_row = _dot_f32(jnp.where(row8 >= MLSTM_HEADS, _log_sigmoid(grow), 0.0), triu)

        ym = []
        for pr in range(MLSTM_HEADS // 2):
            psl = slice(LANE * pr, LANE * (pr + 1))
            q_pair, k_pair = qc[:, psl], kc[:, psl]
            v_pair = mv_c[:, psl]
            h_heads = []
            for e in range(2):
                hd = 2 * pr + e
                own = lane_lo if e == 0 else jnp.logical_not(lane_lo)
                den_lane = MLSTM_HEAD_DIM if e == 0 else 0
                q_h = jnp.where(own, q_pair, 0.0).astype(BF16)
                k_h = jnp.where(own, k_pair, 0.0).astype(BF16)
                v_aug = jnp.where(own, v_pair, jnp.where(lane == den_lane, 1.0, 0.0))
                bc = b_col[:, MLSTM_HEADS + hd:MLSTM_HEADS + hd + 1]
                lic = gcol[:, hd:hd + 1]
                br = b_row[MLSTM_HEADS + hd:MLSTM_HEADS + hd + 1, :]
                lir = grow[hd:hd + 1, :]
                dm = jnp.where(causal, bc + (lir - br), NEG_INF)
                m_prev = m_ref[bb, hd][:, 0:1]
                m_inter = bc + m_prev
                mt = jnp.maximum(m_inter, jnp.max(dm, axis=-1, keepdims=True))
                s = _dot_nt(q_h, k_h) * jnp.exp(dm - mt)
                decay = jnp.exp(m_inter - mt)
                num = (_dot(s.astype(BF16), v_aug.astype(BF16))
                       + decay * _dot(q_h, ct_ref[bb, hd].astype(BF16)))
                den = num[:, den_lane:den_lane + 1]
                h_heads.append(num / jnp.maximum(jnp.abs(den), jnp.exp(-mt)))
                if not update_state:
                    continue
                b_last = bc[T - 1:T, :]
                g_col = b_last - bc + lic
                m_new = jnp.maximum(b_last + m_prev, jnp.max(g_col, axis=0, keepdims=True))
                wg = jnp.exp(g_col - m_new)
                cd = jnp.exp(b_last + m_prev - m_new)
                ct_ref[bb, hd] = cd * ct_ref[bb, hd] + _dot_tn(k_h, (v_aug * wg).astype(BF16))
                m_ref[bb, hd] = jnp.broadcast_to(m_new, (1, LANE))
            h_pair = jnp.where(lane_lo, h_heads[0], h_heads[1])
            sq = h_pair * h_pair
            ms = jnp.where(lane_lo,
                           jnp.sum(jnp.where(lane_lo, sq, 0.0), axis=-1, keepdims=True),
                           jnp.sum(jnp.where(lane_lo, 0.0, sq), axis=-1, keepdims=True)) / MLSTM_HEAD_DIM
            hn = h_pair * lax.rsqrt(ms + EPS) * mn_ref[:, psl]
            ym.append(hn * jax.nn.sigmoid(mo_c[:, psl]))
        store(yp, jnp.concatenate(ym, axis=-1))

    T = SEQ_CHUNK
    TT = SEQ_TAIL
    tail = L - n_full * T

    def store_rows(bb, rows, n=None):
        def store(yp, ym):
            yp_ref[bb, rows, :] = (yp if n is None else yp[:n]).astype(yp_ref.dtype)
            ym_ref[bb, rows, :] = (ym if n is None else ym[:n]).astype(ym_ref.dtype)
        return store

    def head(ref, bb, width):
        return jnp.concatenate([jnp.zeros((HIST, width), F32), ref[bb, 0:T, :].astype(F32)], axis=0)

    for bb in range(n_batch):
        chunk(T, bb, zs_ref[bb, 0:T, :], head(xp_ref, bb, POOL_WIDTH),
              head(mqk_ref, bb, 2 * MLSTM_WIDTH), mv_ref[bb, 0:T, :].astype(F32),
              mo_ref[bb, 0:T, :].astype(F32), 0, store_rows(bb, slice(0, T)))

    def body(c, carry):
        r0 = pl.multiple_of(T * c, T)
        ext = pl.ds(pl.multiple_of(r0 - HIST, 16), T + HIST)
        cur = pl.ds(r0, T)
        for bb in range(n_batch):
            chunk(T, bb, zs_ref[bb, cur, :], xp_ref[bb, ext, :].astype(F32),
                  mqk_ref[bb, ext, :].astype(F32), mv_ref[bb, cur, :].astype(F32),
                  mo_ref[bb, cur, :].astype(F32), r0, store_rows(bb, cur))
        return carry

    lax.fori_loop(1, n_full, body, 0)

    r0 = n_full * T

    def padded(ref, bb, width, lo):
        return jnp.concatenate([ref[bb, lo:L, :].astype(F32), jnp.zeros((TT - tail, width), F32)], axis=0)

    for bb in range(n_batch):
        closed = jnp.broadcast_to(jnp.where(lane < MLSTM_HEADS, NEG_INF, 1e4), (TT - tail, LANE))
        chunk(TT, bb, jnp.concatenate([zs_ref[bb, r0:L, :], closed], axis=0),
              padded(xp_ref, bb, POOL_WIDTH, r0 - HIST),
              padded(mqk_ref, bb, 2 * MLSTM_WIDTH, r0 - HIST), padded(mv_ref, bb, MLSTM_WIDTH, r0),
              padded(mo_ref, bb, MLSTM_WIDTH, r0), r0, store_rows(bb, slice(r0, L), tail),
              update_state=False)


def _seq_mixer(xp, mqk, mv, mo, zs, pw, ps, cw, cb, gbc, mn):
    B, L, _ = xp.shape
    n_full = L // SEQ_CHUNK
    assert 0 < L - n_full * SEQ_CHUNK <= SEQ_TAIL
    BB = SEQ_BATCH
    assert B % BB == 0
    seq = lambda c: pl.BlockSpec((BB, L, c), lambda b: (b, 0, 0))
    full = lambda a: pl.BlockSpec(a.shape, lambda b: (0,) * a.ndim)
    return pl.pallas_call(
        functools.partial(_seq_kernel, n_full=n_full, n_batch=BB),
        grid=(B // BB,),
        in_specs=[seq(POOL_WIDTH), seq(2 * MLSTM_WIDTH), seq(MLSTM_WIDTH), seq(MLSTM_WIDTH),
                  seq(LANE), full(pw), full(ps), full(cw), full(cb), full(gbc), full(mn)],
        out_specs=[seq(POOL_WIDTH), seq(MLSTM_WIDTH)],
        out_shape=[jax.ShapeDtypeStruct((B, L, POOL_WIDTH), BF16),
                   jax.ShapeDtypeStruct((B, L, MLSTM_WIDTH), BF16)],
        scratch_shapes=[pltpu.VMEM((BB, MLSTM_HEADS, LANE, LANE), F32),
                        pltpu.VMEM((BB, MLSTM_HEADS, 1, LANE), F32)],
        compiler_params=_cparams(("parallel",)),
        name="seq_mixer",
    )(xp, mqk, mv, mo, zs, pw, ps, cw, cb, gbc, mn)


def _ffn_kernel(h_ref, ya_ref, yp_ref, ym_ref, wo_ref, g_ref, wgu_ref, wd_ref,
                o_ref, xn_ref, hid_ref):
    hf = h_ref[...] + _dot(jnp.concatenate([ya_ref[...], yp_ref[...], ym_ref[...]], axis=1), wo_ref[...])
    o_ref[...] = hf
    xn_ref[...] = _rms(hf, g_ref[...]).astype(BF16)
    ff = hid_ref.shape[1]
    ft = FF_TILE_DENSE
    for c0 in range(0, ff, ft):
        au = _dot(xn_ref[...], wgu_ref[:, 2 * c0:2 * (c0 + ft)])
        a, u = au[:, :ft], au[:, ft:]
        hid_ref[:, c0:c0 + ft] = (a * jax.nn.sigmoid(a) * u).astype(BF16)
    o_ref[...] += _dot(hid_ref[...], wd_ref[...])


def _ffn(h, ya, yp, ym, wo, g, wgu, wd):
    N, D = h.shape
    FF = wd.shape[0]
    T = _pick_tile(N, FFN_ROW_TILE)
    assert FF % FF_TILE_DENSE == 0
    row = lambda a: pl.BlockSpec((T, a.shape[1]), lambda i: (i, 0))
    full = lambda a: pl.BlockSpec(a.shape, lambda i: (0, 0), pipeline_mode=pl.Buffered(1))
    return pl.pallas_call(
        _ffn_kernel,
        grid=(N // T,),
        in_specs=[row(h), row(ya), row(yp), row(ym), full(wo), full(g),
                  full(wgu), full(wd)],
        out_specs=row(h),
        out_shape=jax.ShapeDtypeStruct((N, D), F32),
        scratch_shapes=[pltpu.VMEM((T, D), BF16), pltpu.VMEM((T, FF), BF16)],
        compiler_params=_cparams(("parallel",)),
        name="dense_ffn",
    )(h, ya, yp, ym, wo, g, wgu, wd)


def _router_kernel(h_ref, ya_ref, yp_ref, ym_ref, wo_ref, g_ref, wrh_ref, wrl_ref, tri_ref,
                   hf_ref, route_ref, cnt_ref, run_ref):
    i = pl.program_id(0)
    T = h_ref.shape[0]

    @pl.when(i == 0)
    def _():
        run_ref[...] = jnp.zeros_like(run_ref)

    hf = h_ref[...] + _dot(jnp.concatenate([ya_ref[...], yp_ref[...], ym_ref[...]], axis=1), wo_ref[...])
    hf_ref[...] = hf
    xn = _rms(hf, g_ref[...])
    lane = lax.broadcasted_iota(jnp.int32, (T, LANE), 1).astype(F32)
    x_hi = xn.astype(BF16)
    x_lo = (xn - x_hi.astype(F32)).astype(BF16)
    logits = _dot(x_hi, wrh_ref[...]) + (_dot(x_lo, wrh_ref[...]) + _dot(x_hi, wrl_ref[...]))
    logits = jnp.where(lane < N_EXPERTS, logits, NEG_INF)
    v1 = jnp.max(logits, axis=-1, keepdims=True)
    i1 = jnp.min(jnp.where(logits == v1, lane, float(LANE)), axis=-1, keepdims=True)
    rest = jnp.where(lane == i1, NEG_INF, logits)
    v2 = jnp.max(rest, axis=-1, keepdims=True)
    i2 = jnp.min(jnp.where(rest == v2, lane, float(LANE)), axis=-1, keepdims=True)
    e2 = jnp.exp(v2 - v1)
    g1 = 1.0 / (1.0 + e2)
    g2 = e2 / (1.0 + e2)
    member = jnp.logical_or(lane == i1, lane == i2).astype(F32)
    rank = run_ref[0:1, :] + _dot(tri_ref[...], member.astype(BF16))
    r1 = jnp.sum(jnp.where(lane == i1, rank, 0.0), axis=-1, keepdims=True)
    r2 = jnp.sum(jnp.where(lane == i2, rank, 0.0), axis=-1, keepdims=True)
    run_ref[...] = run_ref[...] + jnp.sum(member, axis=0, keepdims=True)
    cnt_ref[...] = run_ref[...]
    route = jnp.where(lane == 0, i1, jnp.where(lane == 1, i2,
            jnp.where(lane == 2, r1, jnp.where(lane == 3, r2,
            jnp.where(lane == 4, g1, jnp.where(lane == 5, g2, 0.0))))))
    route_ref[...] = route


def _router(h, ya, yp, ym, wo, g, wr):
    N, D = h.shape
    T = _pick_tile(N, FLAT_TILE)
    tri = jnp.tril(jnp.ones((T, T), BF16), -1)
    wr_hi = wr.astype(BF16)
    wr_lo = (wr - wr_hi.astype(F32)).astype(BF16)
    row = lambda a: pl.BlockSpec((T, a.shape[1]), lambda i: (i, 0))
    full = lambda a: pl.BlockSpec(a.shape, lambda i: (0, 0))
    return pl.pallas_call(
        _router_kernel,
        grid=(N // T,),
        in_specs=[row(h), row(ya), row(yp), row(ym), full(wo), full(g), full(wr_hi),
                  full(wr_lo), full(tri)],
        out_specs=[pl.BlockSpec((T, D), lambda i: (i, 0)),
                   pl.BlockSpec((T, LANE), lambda i: (i, 0)),
                   pl.BlockSpec((8, LANE), lambda i: (0, 0))],
        out_shape=[jax.ShapeDtypeStruct((N, D), F32), jax.ShapeDtypeStruct((N, LANE), F32),
                   jax.ShapeDtypeStruct((8, LANE), F32)],
        scratch_shapes=[pltpu.VMEM((8, LANE), F32)],
        compiler_params=_cparams(("arbitrary",)),
        name="router",
    )(h, ya, yp, ym, wo, g, wr_hi, wr_lo, tri)


def _dispatch_kernel(seg_ref, dest_ref, h_ref, xs_ref, zero_ref, sem):
    n = h_ref.shape[0]
    zb = zero_ref.shape[0]
    n_rows = xs_ref.shape[0]

    def pad_row(r):
        return pltpu.make_async_copy(zero_ref.at[pl.ds(0, 1)], xs_ref.at[pl.ds(r, 1)], sem)

    def tail_block(b):
        start = pl.multiple_of(seg_ref[2 * N_EXPERTS - 1] + b * zb, zb)
        return pltpu.make_async_copy(zero_ref, xs_ref.at[pl.ds(start, zb)], sem)

    def for_unwritten(start):
        for e in range(N_EXPERTS):
            lax.fori_loop(seg_ref[e], seg_ref[N_EXPERTS + e],
                          lambda r, c: (pad_row(r).start() if start else pad_row(r).wait(), c)[1], 0)
        lax.fori_loop(0, (n_rows - seg_ref[2 * N_EXPERTS - 1]) // zb,
                      lambda b, c: (tail_block(b).start() if start else tail_block(b).wait(), c)[1], 0)

    @pl.when(pl.program_id(0) == 0)
    def _():
        zero_ref[...] = jnp.zeros_like(zero_ref)
        for_unwritten(True)
        for_unwritten(False)

    def copy(t, slot):
        return pltpu.make_async_copy(h_ref.at[pl.ds(t, 1)],
                                     xs_ref.at[pl.ds(dest_ref[0, 0, 2 * t + slot], 1)], sem)

    def issue(t, c):
        copy(t, 0).start(priority=0)
        copy(t, 1).start(priority=1)
        return c

    lax.fori_loop(0, n, issue, 0, unroll=8)

    def drain(t, c):
        copy(t, 0).wait()
        copy(t, 1).wait()
        return c

    lax.fori_loop(0, n, drain, 0, unroll=8)


def _dispatch(seg, dest3, h, n_rows):
    n_tiles = dest3.shape[0]
    T, D = dest3.shape[2] // 2, h.shape[1]
    grid_spec = pltpu.PrefetchScalarGridSpec(
        num_scalar_prefetch=1,
        grid=(n_tiles,),
        in_specs=[pl.BlockSpec((1, 1, 2 * T), lambda i, seg: (i, 0, 0), memory_space=pltpu.SMEM),
                  pl.BlockSpec((T, D), lambda i, seg: (i, 0))],
        out_specs=pl.BlockSpec(memory_space=pl.ANY),
        scratch_shapes=[pltpu.VMEM((ZERO_ROWS, D), F32), pltpu.SemaphoreType.DMA(())],
    )
    return pl.pallas_call(
        _dispatch_kernel,
        grid_spec=grid_spec,
        out_shape=jax.ShapeDtypeStruct((n_rows, D), F32),
        compiler_params=_cparams(("arbitrary",)),
        name="dispatch",
    )(seg, dest3, h)


def _expert_kernel(be_ref, x_ref, g_ref, wgu_ref, wd_ref, o_ref, xn_ref):
    r = pl.program_id(0)
    j = pl.program_id(1)
    n_active = be_ref[be_ref.shape[0] - 1]

    @pl.when(r < n_active)
    def _():
        @pl.when(j == 0)
        def _():
            xn_ref[...] = _rms(x_ref[...], g_ref[...]).astype(BF16)

        tf = wd_ref.shape[1]
        au = _dot(xn_ref[...], wgu_ref[0])
        a, u = au[:, :tf], au[:, tf:]
        y = _dot((a * jax.nn.sigmoid(a) * u).astype(BF16), wd_ref[0])

        @pl.when(j == 0)
        def _():
            o_ref[...] = y

        @pl.when(j > 0)
        def _():
            o_ref[...] += y

    @pl.when(jnp.logical_and(r >= n_active, j == 0))
    def _():
        o_ref[...] = jnp.zeros_like(o_ref)


def _experts(block_exp, xs, g, wgu, wd):
    R, D = xs.shape
    TM = MOE_TILE
    FF = wd.shape[1]
    TF = FF_TILE_EXPERT
    assert FF % TF == 0 and R % TM == 0
    grid_spec = pltpu.PrefetchScalarGridSpec(
        num_scalar_prefetch=1,
        grid=(R // TM, FF // TF),
        in_specs=[pl.BlockSpec((TM, D), lambda r, j, be: (r, 0)),
                  pl.BlockSpec(g.shape, lambda r, j, be: (0, 0)),
                  pl.BlockSpec((1, D, 2 * TF), lambda r, j, be: (be[r], 0, j)),
                  pl.BlockSpec((1, TF, D), lambda r, j, be: (be[r], j, 0))],
        out_specs=pl.BlockSpec((TM, D), lambda r, j, be: (r, 0)),
        scratch_shapes=[pltpu.VMEM((TM, D), BF16)],
    )
    return pl.pallas_call(
        _expert_kernel,
        grid_spec=grid_spec,
        out_shape=jax.ShapeDtypeStruct((R, D), F32),
        compiler_params=_cparams(("parallel", "arbitrary")),
        name="experts",
    )(block_exp, xs, g, wgu, wd)


def _combine_kernel(dest_ref, next_ref, h_ref, route_ref, ys_ref, o_ref, buf_ref, sems):
    i = pl.program_id(0)
    T = h_ref.shape[0]
    cur = i % 2

    def copy(idx_ref, b, t, slot):
        return pltpu.make_async_copy(ys_ref.at[pl.ds(idx_ref[0, 0, 2 * t + slot], 1)],
                                     buf_ref.at[b, slot, pl.ds(t, 1)], sems.at[b])

    def issue(idx_ref, b):
        def body(t, c):
            copy(idx_ref, b, t, 0).start(priority=0)
            copy(idx_ref, b, t, 1).start(priority=1)
            return c
        lax.fori_loop(0, T, body, 0, unroll=8)

    @pl.when(i == 0)
    def _():
        issue(dest_ref, 0)

    @pl.when(i + 1 < pl.num_programs(0))
    def _():
        issue(next_ref, 1 - cur)

    def drain(t, c):
        copy(dest_ref, cur, t, 0).wait()
        copy(dest_ref, cur, t, 1).wait()
        return c

    lax.fori_loop(0, T, drain, 0, unroll=8)
    route = route_ref[...]
    o_ref[...] = h_ref[...] + route[:, 4:5] * buf_ref[cur, 0] + route[:, 5:6] * buf_ref[cur, 1]


def _combine(dest3, h, route, ys):
    N, D = h.shape
    T = dest3.shape[2] // 2
    n_tiles = N // T
    idx_spec = lambda f: pl.BlockSpec((1, 1, 2 * T), lambda i: (f(i), 0, 0), memory_space=pltpu.SMEM)
    return pl.pallas_call(
        _combine_kernel,
        grid=(n_tiles,),
        in_specs=[idx_spec(lambda i: i), idx_spec(lambda i: jnp.minimum(i + 1, n_tiles - 1)),
                  pl.BlockSpec((T, D), lambda i: (i, 0)),
                  pl.BlockSpec((T, LANE), lambda i: (i, 0)),
                  pl.BlockSpec(memory_space=pl.ANY)],
        out_specs=pl.BlockSpec((T, D), lambda i: (i, 0)),
        out_shape=jax.ShapeDtypeStruct((N, D), F32),
        scratch_shapes=[pltpu.VMEM((2, 2, T, D), F32), pltpu.SemaphoreType.DMA((2,))],
        compiler_params=_cparams(("arbitrary",)),
        name="combine",
    )(dest3, dest3, h, route, ys)


def _moe(mixed, g, wr, wgu, wd):
    N, D = mixed[0].shape
    T = _pick_tile(N, FLAT_TILE)
    h, route, counts = _router(*mixed, g, wr)
    cnt = counts[0, :N_EXPERTS].astype(jnp.int32)
    padded = (cnt + MOE_TILE - 1) // MOE_TILE * MOE_TILE
    pend = jnp.cumsum(padded)
    pstart = pend - padded
    idx = route[:, 0:2].astype(jnp.int32)
    dest = pstart[idx] + route[:, 2:4].astype(jnp.int32)
    dest3 = dest.reshape(N // T, 1, 2 * T)
    n_blocks = -(-2 * N // MOE_TILE) + N_EXPERTS
    block_exp = jnp.minimum(jnp.searchsorted(pend, jnp.arange(n_blocks) * MOE_TILE, side='right'),
                            N_EXPERTS - 1).astype(jnp.int32)
    block_exp = jnp.concatenate([block_exp, (pend[-1:] // MOE_TILE).astype(jnp.int32)])
    seg = jnp.concatenate([pstart + cnt, pend]).astype(jnp.int32)
    xs = _dispatch(seg, dest3, h, n_blocks * MOE_TILE)
    ys = _experts(block_exp, xs, g, wgu, wd)
    return _combine(dest3, h, route, ys)


def _final_kernel(h_ref, g_ref, o_ref):
    o_ref[0] = _rms(h_ref[0, N_META:, :], g_ref[...])


def _final(h, g):
    B, L, D = h.shape
    return pl.pallas_call(
        _final_kernel,
        grid=(B,),
        in_specs=[pl.BlockSpec((1, L, D), lambda b: (b, 0, 0)), pl.BlockSpec(g.shape, lambda b: (0, 0))],
        out_specs=pl.BlockSpec((1, L - N_META, D), lambda b: (b, 0, 0)),
        out_shape=jax.ShapeDtypeStruct((B, L - N_META, D), F32),
        compiler_params=_cparams(("parallel",)),
        name="final_norm",
    )(h, g)


def _rope_tables(L):
    pos = jnp.arange(L, dtype=jnp.int32)
    inv_freq = ROPE_THETA ** (-jnp.arange(0, QK_ROPE, 2, dtype=F32) / QK_ROPE)
    ang = pos.astype(F32)[:, None] * inv_freq[None, :]
    cos, sin = jnp.cos(ang), jnp.sin(ang)
    z = lambda n: jnp.zeros((L, n), F32)
    scale = (QK_NOPE + QK_ROPE) ** -0.5 * float(np.log2(np.e))
    cq = jnp.concatenate([jnp.ones((L, QK_NOPE), F32), cos, cos, z(32)], axis=1) * scale
    s1q = jnp.concatenate([z(QK_NOPE), -sin, z(48)], axis=1) * scale
    s2q = jnp.concatenate([z(QK_NOPE + 16), sin, z(32)], axis=1) * scale
    g = 2 * MLSTM_HEADS
    ck = jnp.concatenate([z(g), cos, cos, z(96 - g)], axis=1)
    s1k = jnp.concatenate([z(g), -sin, z(112 - g)], axis=1)
    s2k = jnp.concatenate([z(g + 16), sin, z(96 - g)], axis=1)
    return jnp.concatenate([cq, s1q, s2q, ck, s1k, s2k], axis=1)


def _interleave_tiles(wg, wu, tile):
    pieces = []
    for c0 in range(0, wg.shape[-1], tile):
        pieces += [wg[..., c0:c0 + tile], wu[..., c0:c0 + tile]]
    return jnp.concatenate(pieces, axis=-1).astype(BF16)


def _layer_weights(l, w_in, w_q_up, w_kv_up, pool_w, w_out):
    pts = np.cumsum([Q_LORA, KV_LORA, QK_ROPE, POOL_WIDTH, 2 * MLSTM_WIDTH, MLSTM_WIDTH, MLSTM_WIDTH]).tolist()
    wi = w_in[l]
    w_cq, w_ckv, w_kr, w_xp, w_mqk, w_mv, w_mo, w_mg = jnp.split(wi, pts, axis=1)
    small = jnp.concatenate([w_mg, w_kr, jnp.zeros((D_MODEL, LANE - QK_ROPE - 2 * MLSTM_HEADS), F32)], axis=1)
    wbig = jnp.concatenate([w_cq, w_ckv, w_xp, w_mqk, w_mv, w_mo, small], axis=1).astype(BF16)
    wq = w_q_up[l].reshape(Q_LORA, MLA_HEADS, QK_NOPE + QK_ROPE)
    wq = jnp.pad(wq, ((0, 0), (0, 0), (0, HEAD_PAD - QK_NOPE - QK_ROPE))).reshape(Q_LORA, -1).astype(BF16)
    wkv = w_kv_up[l].reshape(KV_LORA, MLA_HEADS, QK_NOPE + V_HEAD)
    wk = jnp.pad(wkv[:, :, :QK_NOPE], ((0, 0), (0, 0), (0, HEAD_PAD - QK_NOPE))).reshape(KV_LORA, -1).astype(BF16)
    wv = wkv[:, :, QK_NOPE:].reshape(KV_LORA, -1).astype(BF16)
    pw = jax.scipy.linalg.block_diag(*[pool_w[l, g] for g in range(POOL_GROUPS)]).astype(BF16)
    return dict(wbig=wbig, wq=wq, wk=wk, wv=wv, pw=pw, wo=w_out[l].astype(BF16))


def kernel(x, meta_tokens, attn_norm, w_in, q_norm, w_q_up, kv_norm, w_kv_up, pool_w, pool_scale, conv_w, conv_b, gate_bias, mlstm_norm, w_out, ffn_norm, dense_w_gate, dense_w_up, dense_w_down, router_w, moe_w_gate, moe_w_up, moe_w_down, final_norm):
    B, S, D = x.shape
    L = N_META + S
    N = B * L
    depth = w_in.shape[0]
    n_full = L // SEQ_CHUNK
    assert S % ATTN_TILE == 0 and S % SEQ_CHUNK == 0

    meta = jnp.broadcast_to(meta_tokens[None].astype(x.dtype), (B, N_META, D))
    h = jnp.concatenate([meta, x], axis=1)
    tab = _rope_tables(L)
    row = lambda a: a.reshape(1, -1).astype(F32)

    for l in range(depth):
        lw = _layer_weights(l, w_in, w_q_up, w_kv_up, pool_w, w_out)
        q, k, v, xp, mqk, mv, mo, zs = _inproj(h, row(attn_norm[l]), lw['wbig'], row(q_norm[l]), lw['wq'],
                                               row(kv_norm[l]), lw['wk'], lw['wv'], tab)
        ya = _attention(q, k, v)
        gbc = jnp.pad(gate_bias[l], (0, LANE - 2 * MLSTM_HEADS)).reshape(1, LANE).astype(F32)
        yp, ym = _seq_mixer(xp, mqk, mv, mo, zs, lw['pw'], row(pool_scale[l]),
                            conv_w[l].astype(F32), row(conv_b[l]), gbc, row(mlstm_norm[l]))
        mixed = (h.reshape(N, D), ya.reshape(N, -1), yp.reshape(N, -1), ym.reshape(N, -1),
                 lw['wo'])
        j = l // 2
        if l % 2 == 0:
            wgu = _interleave_tiles(dense_w_gate[j], dense_w_up[j], FF_TILE_DENSE)
            hf = _ffn(*mixed, row(ffn_norm[l]), wgu, dense_w_down[j].astype(BF16))
        else:
            wr = jnp.pad(router_w[j], ((0, 0), (0, LANE - N_EXPERTS))).astype(F32)
            wgu = _interleave_tiles(moe_w_gate[j], moe_w_up[j], FF_TILE_EXPERT)
            hf = _moe(mixed, row(ffn_norm[l]), wr, wgu, moe_w_down[j].astype(BF16))
        h = hf.reshape(B, L, D)
    return _final(h, row(final_norm))
```

```python
import functools

import numpy as np
import jax
import jax.numpy as jnp
from jax import lax
from jax.experimental import pallas as pl
from jax.experimental.pallas import tpu as pltpu

F32 = jnp.float32
BF16 = jnp.bfloat16

D_MODEL = 1024
N_META = 16
CHUNK = 64
MLA_HEADS = 8
QK_NOPE = 64
QK_ROPE = 32
V_HEAD = 64
Q_LORA = 768
KV_LORA = 256
ROPE_THETA = 10000.0
POOL_GROUPS = 4
POOL_GROUP_DIM = 64
POOL_WIDTH = 256
MLSTM_HEADS = 4
MLSTM_HEAD_DIM = 64
MLSTM_WIDTH = 256
CONV_WIDTH = 4
D_FF = 2816
N_EXPERTS = 8
D_FF_EXPERT = 3584
EPS = 1e-6
NEG_INF = -1e30

LANE = 128
HEAD_PAD = 128
ATTN_TILE = 256
SEQ_CHUNK = 256
SEQ_TAIL = 128
SEQ_BATCH = 2
FLAT_TILE = 512
FFN_ROW_TILE = 768
MOE_TILE = 512
ZERO_ROWS = 256
FF_TILE_DENSE = 1408
FF_TILE_EXPERT = 1792
VMEM_LIMIT = 56 * 1024 * 1024

SEG_CQ = (0, 768)
SEG_CKV = (768, 1024)
SEG_XP = (1024, 1280)
SEG_MQK = (1280, 1792)
SEG_MV = (1792, 2048)
SEG_MO = (2048, 2304)
SEG_SMALL = (2304, 2432)
W_BIG = 2432


def _rms(x, g):
    ms = jnp.mean(x * x, axis=-1, keepdims=True)
    return x * lax.rsqrt(ms + EPS) * g


def _dot(a, b):
    return jnp.dot(a, b, preferred_element_type=F32)


def _dot_nt(a, b):
    return lax.dot_general(a, b, (((1,), (1,)), ((), ())), preferred_element_type=F32)


def _dot_tn(a, b):
    return lax.dot_general(a, b, (((0,), (0,)), ((), ())), preferred_element_type=F32)


def _dot_f32(a, b):
    return jnp.dot(a, b, preferred_element_type=F32, precision=lax.Precision.HIGHEST)


def _log_sigmoid(x):
    return jnp.minimum(x, 0.0) - jnp.log(1.0 + jnp.exp(-jnp.abs(x)))


def _pick_tile(n, target):
    best = None
    for t in range(16, target + 1, 16):
        if n % t == 0:
            best = t
    assert best is not None, (n, target)
    return best


def _cparams(sem):
    return pltpu.CompilerParams(dimension_semantics=sem, vmem_limit_bytes=VMEM_LIMIT)


def _rope_lanes(x, c, s_lo, s_hi):
    return x * c + pltpu.roll(x, LANE - 16, 1) * s_lo + pltpu.roll(x, 16, 1) * s_hi


def _inproj_kernel(h_ref, an_ref, wbig_ref, qn_ref, wq_ref, kvn_ref, wk_ref, wv_ref, tab_ref,
                   q_ref, k_ref, v_ref, xp_ref, mqk_ref, mv_ref, mo_ref, zs_ref):
    xn = _rms(h_ref[0], an_ref[...]).astype(BF16)

    z = _dot(xn, wbig_ref[...])

    def seg(ab):
        return z[:, ab[0]:ab[1]]

    xp_ref[0] = seg(SEG_XP).astype(BF16)
    mqk_ref[0] = seg(SEG_MQK).astype(BF16)
    mv_ref[0] = seg(SEG_MV).astype(BF16)
    mo_ref[0] = seg(SEG_MO).astype(BF16)
    zs = seg(SEG_SMALL)
    zs_ref[0] = zs

    tab = tab_ref[...]
    cq, s1q, s2q = tab[:, 0:128], tab[:, 128:256], tab[:, 256:384]
    ck, s1k, s2k = tab[:, 384:512], tab[:, 512:640], tab[:, 640:768]

    cqn = _rms(seg(SEG_CQ), qn_ref[...]).astype(BF16)
    qf = _dot(cqn, wq_ref[...])
    for hd in range(MLA_HEADS):
        sl = slice(HEAD_PAD * hd, HEAD_PAD * (hd + 1))
        q_ref[0, hd] = _rope_lanes(qf[:, sl], cq, s1q, s2q).astype(BF16)

    ckvn = _rms(seg(SEG_CKV), kvn_ref[...]).astype(BF16)
    v_ref[0] = _dot(ckvn, wv_ref[...]).astype(BF16)
    kf = _dot(ckvn, wk_ref[...])
    kr = pltpu.roll(_rope_lanes(zs, ck, s1k, s2k), QK_NOPE - 2 * MLSTM_HEADS, 1)
    for hd in range(MLA_HEADS):
        sl = slice(HEAD_PAD * hd, HEAD_PAD * (hd + 1))
        k_ref[0, hd] = (kf[:, sl] + kr).astype(BF16)


def _inproj(h, an, wbig, qn, wq, kvn, wk, wv, tab):
    B, L, D = h.shape
    T = _pick_tile(L, 768)
    nt = L // T
    row = lambda c: pl.BlockSpec((1, T, c), lambda b, i: (b, i, 0))
    full = lambda a: pl.BlockSpec(a.shape, lambda b, i: (0,) * a.ndim)
    widths = (MLA_HEADS * V_HEAD, POOL_WIDTH, 2 * MLSTM_WIDTH, MLSTM_WIDTH, MLSTM_WIDTH)
    head_major = jax.ShapeDtypeStruct((B, MLA_HEADS, L, HEAD_PAD), BF16)
    head_spec = pl.BlockSpec((1, MLA_HEADS, T, HEAD_PAD), lambda b, i: (b, 0, i, 0))
    out_shape = [head_major, head_major] + [jax.ShapeDtypeStruct((B, L, c), BF16) for c in widths]
    out_shape.append(jax.ShapeDtypeStruct((B, L, LANE), F32))
    out_specs = [head_spec, head_spec] + [row(c) for c in widths] + [row(LANE)]
    return pl.pallas_call(
        _inproj_kernel,
        grid=(B, nt),
        in_specs=[row(D), full(an), full(wbig), full(qn), full(wq), full(kvn), full(wk), full(wv),
                  pl.BlockSpec((T, tab.shape[1]), lambda b, i: (i, 0))],
        out_specs=out_specs,
        out_shape=out_shape,
        compiler_params=_cparams(("parallel", "parallel")),
        name="inproj",
    )(h, an, wbig, qn, wq, kvn, wk, wv, tab)


def _attn_kernel(q_ref, k_ref, v_ref, o_ref, vm_ref, s_ref, p_ref, oacc_ref, *, n_tiles):
    A = ATTN_TILE
    lane = lax.broadcasted_iota(jnp.int32, (1, LANE), 1)
    ri = lax.broadcasted_iota(jnp.int32, (A, A), 0) // CHUNK
    ci = lax.broadcasted_iota(jnp.int32, (A, A), 1) // CHUNK
    diag_mask = ci <= ri
    oacc_ref[...] = jnp.zeros_like(oacc_ref)

    def head_body(hh, carry):
        vm_ref[...] = jnp.where(lane // V_HEAD == hh, v_ref[0], jnp.zeros_like(v_ref[0]))

        s0 = _dot_nt(q_ref[0, hh, 0:N_META, :], k_ref[0, hh, 0:N_META, :])
        p0 = jnp.exp2(s0 - jnp.max(s0, axis=-1, keepdims=True))
        o0 = _dot(p0.astype(BF16), vm_ref[0:N_META, :]) / jnp.sum(p0, axis=-1, keepdims=True)
        oacc_ref[0:N_META, :] += o0

        for j in range(n_tiles):
            r0 = N_META + A * j
            qj = q_ref[0, hh, r0:r0 + A, :]
            s_meta = _dot_nt(qj, k_ref[0, hh, 0:N_META, :])
            m128 = None
            for c in range(j + 1):
                s = _dot_nt(qj, k_ref[0, hh, N_META + A * c:N_META + A * (c + 1), :])
                if c == j:
                    s = jnp.where(diag_mask, s, NEG_INF)
                s_ref[:, A * c:A * (c + 1)] = s
                mc = jnp.maximum(s[:, :LANE], s[:, LANE:])
                m128 = mc if m128 is None else jnp.maximum(m128, mc)
            m = jnp.maximum(jnp.max(m128, axis=-1, keepdims=True),
                            jnp.max(s_meta, axis=-1, keepdims=True))
            p_meta = jnp.exp2(s_meta - m)
            l128 = jnp.zeros((A, LANE), F32)
            for c in range(j + 1):
                p = jnp.exp2(s_ref[:, A * c:A * (c + 1)] - m)
                l128 = l128 + p[:, :LANE] + p[:, LANE:]
                p_ref[:, A * c:A * (c + 1)] = p.astype(BF16)
            l = jnp.sum(p_meta, axis=-1, keepdims=True) + jnp.sum(l128, axis=-1, keepdims=True)
            n = A * (j + 1)
            acc = (_dot(p_ref[:, 0:n], vm_ref[N_META:N_META + n, :])
                   + _dot(p_meta.astype(BF16), vm_ref[0:N_META, :]))
            oacc_ref[r0:r0 + A, :] += acc / l
        return carry

    lax.fori_loop(0, 2, head_body, 0)
    o_ref[0] = oacc_ref[...].astype(o_ref.dtype)


def _attention(q, k, v):
    B, _, L, _ = q.shape
    n_tiles = (L - N_META) // ATTN_TILE
    assert N_META + n_tiles * ATTN_TILE == L
    n_pairs = MLA_HEADS // 2
    qk_spec = pl.BlockSpec((1, 2, L, HEAD_PAD), lambda b, p: (b, p, 0, 0))
    v_spec = pl.BlockSpec((1, L, 2 * V_HEAD), lambda b, p: (b, 0, p))
    return pl.pallas_call(
        functools.partial(_attn_kernel, n_tiles=n_tiles),
        grid=(B, n_pairs),
        in_specs=[qk_spec, qk_spec, v_spec],
        out_specs=v_spec,
        out_shape=jax.ShapeDtypeStruct((B, L, MLA_HEADS * V_HEAD), BF16),
        scratch_shapes=[pltpu.VMEM((L, 2 * V_HEAD), BF16),
                        pltpu.VMEM((ATTN_TILE, n_tiles * ATTN_TILE), F32),
                        pltpu.VMEM((ATTN_TILE, n_tiles * ATTN_TILE), BF16),
                        pltpu.VMEM((L, 2 * V_HEAD), F32)],
        compiler_params=_cparams(("parallel", "parallel")),
        name="attention",
    )(q, k, v)


def _seq_kernel(xp_ref, mqk_ref, mv_ref, mo_ref, zs_ref, pw_ref, ps_ref,
                cw_ref, cb_ref, gbc_ref, mn_ref, yp_ref, ym_ref, ct_ref, m_ref, *, n_full, n_batch):
    HIST = 16
    L = xp_ref.shape[1]
    lane = lax.broadcasted_iota(jnp.int32, (1, LANE), 1)
    lane_lo = lane < MLSTM_HEAD_DIM
    lane256 = lax.broadcasted_iota(jnp.int32, (1, POOL_WIDTH), 1)
    grp = lane256 // POOL_GROUP_DIM
    win = jnp.left_shift(2, grp).astype(F32)
    row8 = lax.broadcasted_iota(jnp.int32, (2 * MLSTM_HEADS, 1), 0)

    ct_ref[...] = jnp.zeros_like(ct_ref)
    m_ref[...] = jnp.zeros_like(m_ref)

    def chunk(T, bb, gcol_raw, xx_p, xx_c, mv_c, mo_c, pos0, store, update_state=True):
        ti = lax.broadcasted_iota(jnp.int32, (T, T), 0)
        si = lax.broadcasted_iota(jnp.int32, (T, T), 1)
        causal = si <= ti
        tril = causal.astype(F32)
        triu = (ti <= si).astype(F32)
        t_col = lax.broadcasted_iota(jnp.int32, (T, 1), 0)
        s2 = xx_p + pltpu.roll(xx_p, 1, 0)
        s4 = s2 + pltpu.roll(s2, 2, 0)
        s8 = s4 + pltpu.roll(s4, 4, 0)
        s16 = s8 + pltpu.roll(s8, 8, 0)
        ssum = jnp.where(grp == 0, s2[HIST:], jnp.where(grp == 1, s4[HIST:],
                         jnp.where(grp == 2, s8[HIST:], s16[HIST:])))
        cnt = jnp.maximum(pos0 + t_col + 1, 1).astype(F32)
        d = ssum / jnp.minimum(cnt, win) - xx_p[HIST:]
        yp = _dot(d.astype(BF16), pw_ref[...]) * ps_ref[...]

        cw = cw_ref[...]
        y = (xx_c * cw[3:4] + pltpu.roll(xx_c, 1, 0) * cw[2:3] + pltpu.roll(xx_c, 2, 0) * cw[1:2]
             + pltpu.roll(xx_c, 3, 0) * cw[0:1])[HIST:] + cb_ref[...]
        qk = y * jax.nn.sigmoid(y)
        qc = qk[:, :MLSTM_WIDTH]
        kc = qk[:, MLSTM_WIDTH:] * (MLSTM_HEAD_DIM ** -0.5)

        gcol = gcol_raw + gbc_ref[...]
        grow = gcol.T[0:2 * MLSTM_HEADS, :]
        is_f = jnp.logical_and(lane >= MLSTM_HEADS, lane < 2 * MLSTM_HEADS)
        b_col = _dot_f32(tril, jnp.where(is_f, _log_sigmoid(gcol), 0.0))
        b_row = _dot_f32(jnp.where(row8 >= MLSTM_HEADS, _log_sigmoid(grow), 0.0), triu)

        ym = []
        for pr in range(MLSTM_HEADS // 2):
            psl = slice(LANE * pr, LANE * (pr + 1))
            q_pair, k_pair = qc[:, psl], kc[:, psl]
            v_pair = mv_c[:, psl]
            h_heads = []
            for e in range(2):
                hd = 2 * pr + e
                own = lane_lo if e == 0 else jnp.logical_not(lane_lo)
                den_lane = MLSTM_HEAD_DIM if e == 0 else 0
                q_h = jnp.where(own, q_pair, 0.0).astype(BF16)
                k_h = jnp.where(own, k_pair, 0.0).astype(BF16)
                v_aug = jnp.where(own, v_pair, jnp.where(lane == den_lane, 1.0, 0.0))
                bc = b_col[:, MLSTM_HEADS + hd:MLSTM_HEADS + hd + 1]
                lic = gcol[:, hd:hd + 1]
                br = b_row[MLSTM_HEADS + hd:MLSTM_HEADS + hd + 1, :]
                lir = grow[hd:hd + 1, :]
                dm = jnp.where(causal, bc + (lir - br), NEG_INF)
                m_prev = m_ref[bb, hd][:, 0:1]
                m_inter = bc + m_prev
                mt = jnp.maximum(m_inter, jnp.max(dm, axis=-1, keepdims=True))
                s = _dot_nt(q_h, k_h) * jnp.exp(dm - mt)
                decay = jnp.exp(m_inter - mt)
                num = (_dot(s.astype(BF16), v_aug.astype(BF16))
                       + decay * _dot(q_h, ct_ref[bb, hd].astype(BF16)))
                den = num[:, den_lane:den_lane + 1]
                h_heads.append(num / jnp.maximum(jnp.abs(den), jnp.exp(-mt)))
                if not update_state:
                    continue
                b_last = bc[T - 1:T, :]
                g_col = b_last - bc + lic
                m_new = jnp.maximum(b_last + m_prev, jnp.max(g_col, axis=0, keepdims=True))
                wg = jnp.exp(g_col - m_new)
                cd = jnp.exp(b_last + m_prev - m_new)
                ct_ref[bb, hd] = cd * ct_ref[bb, hd] + _dot_tn(k_h, (v_aug * wg).astype(BF16))
                m_ref[bb, hd] = jnp.broadcast_to(m_new, (1, LANE))
            h_pair = jnp.where(lane_lo, h_heads[0], h_heads[1])
            sq = h_pair * h_pair
            ms = jnp.where(lane_lo,
                           jnp.sum(jnp.where(lane_lo, sq, 0.0), axis=-1, keepdims=True),
                           jnp.sum(jnp.where(lane_lo, 0.0, sq), axis=-1, keepdims=True)) / MLSTM_HEAD_DIM
            hn = h_pair * lax.rsqrt(ms + EPS) * mn_ref[:, psl]
            ym.append(hn * jax.nn.sigmoid(mo_c[:, psl]))
        store(yp, jnp.concatenate(ym, axis=-1))

    T = SEQ_CHUNK
    TT = SEQ_TAIL
    tail = L - n_full * T

    def store_rows(bb, rows, n=None):
        def store(yp, ym):
            yp_ref[bb, rows, :] = (yp if n is None else yp[:n]).astype(yp_ref.dtype)
            ym_ref[bb, rows, :] = (ym if n is None else ym[:n]).astype(ym_ref.dtype)
        return store

    def head(ref, bb, width):
        return jnp.concatenate([jnp.zeros((HIST, width), F32), ref[bb, 0:T, :].astype(F32)], axis=0)

    for bb in range(n_batch):
        chunk(T, bb, zs_ref[bb, 0:T, :], head(xp_ref, bb, POOL_WIDTH),
              head(mqk_ref, bb, 2 * MLSTM_WIDTH), mv_ref[bb, 0:T, :].astype(F32),
              mo_ref[bb, 0:T, :].astype(F32), 0, store_rows(bb, slice(0, T)))

    def body(c, carry):
        r0 = pl.multiple_of(T * c, T)
        ext = pl.ds(pl.multiple_of(r0 - HIST, 16), T + HIST)
        cur = pl.ds(r0, T)
        for bb in range(n_batch):
            chunk(T, bb, zs_ref[bb, cur, :], xp_ref[bb, ext, :].astype(F32),
                  mqk_ref[bb, ext, :].astype(F32), mv_ref[bb, cur, :].astype(F32),
                  mo_ref[bb, cur, :].astype(F32), r0, store_rows(bb, cur))
        return carry

    lax.fori_loop(1, n_full, body, 0)

    r0 = n_full * T

    def padded(ref, bb, width, lo):
        return jnp.concatenate([ref[bb, lo:L, :].astype(F32), jnp.zeros((TT - tail, width), F32)], axis=0)

    for bb in range(n_batch):
        closed = jnp.broadcast_to(jnp.where(lane < MLSTM_HEADS, NEG_INF, 1e4), (TT - tail, LANE))
        chunk(TT, bb, jnp.concatenate([zs_ref[bb, r0:L, :], closed], axis=0),
              padded(xp_ref, bb, POOL_WIDTH, r0 - HIST),
              padded(mqk_ref, bb, 2 * MLSTM_WIDTH, r0 - HIST), padded(mv_ref, bb, MLSTM_WIDTH, r0),
              padded(mo_ref, bb, MLSTM_WIDTH, r0), r0, store_rows(bb, slice(r0, L), tail),
              update_state=False)


def _seq_mixer(xp, mqk, mv, mo, zs, pw, ps, cw, cb, gbc, mn):
    B, L, _ = xp.shape
    n_full = L // SEQ_CHUNK
    assert 0 < L - n_full * SEQ_CHUNK <= SEQ_TAIL
    BB = SEQ_BATCH
    assert B % BB == 0
    seq = lambda c: pl.BlockSpec((BB, L, c), lambda b: (b, 0, 0))
    full = lambda a: pl.BlockSpec(a.shape, lambda b: (0,) * a.ndim)
    return pl.pallas_call(
        functools.partial(_seq_kernel, n_full=n_full, n_batch=BB),
        grid=(B // BB,),
        in_specs=[seq(POOL_WIDTH), seq(2 * MLSTM_WIDTH), seq(MLSTM_WIDTH), seq(MLSTM_WIDTH),
                  seq(LANE), full(pw), full(ps), full(cw), full(cb), full(gbc), full(mn)],
        out_specs=[seq(POOL_WIDTH), seq(MLSTM_WIDTH)],
        out_shape=[jax.ShapeDtypeStruct((B, L, POOL_WIDTH), BF16),
                   jax.ShapeDtypeStruct((B, L, MLSTM_WIDTH), BF16)],
        scratch_shapes=[pltpu.VMEM((BB, MLSTM_HEADS, LANE, LANE), F32),
                        pltpu.VMEM((BB, MLSTM_HEADS, 1, LANE), F32)],
        compiler_params=_cparams(("parallel",)),
        name="seq_mixer",
    )(xp, mqk, mv, mo, zs, pw, ps, cw, cb, gbc, mn)


def _ffn_kernel(h_ref, ya_ref, yp_ref, ym_ref, wo_ref, g_ref, wgu_ref, wd_ref,
                o_ref, xn_ref, hid_ref):
    hf = h_ref[...] + _dot(jnp.concatenate([ya_ref[...], yp_ref[...], ym_ref[...]], axis=1), wo_ref[...])
    o_ref[...] = hf
    xn_ref[...] = _rms(hf, g_ref[...]).astype(BF16)
    ff = hid_ref.shape[1]
    ft = FF_TILE_DENSE
    for c0 in range(0, ff, ft):
        au = _dot(xn_ref[...], wgu_ref[:, 2 * c0:2 * (c0 + ft)])
        a, u = au[:, :ft], au[:, ft:]
        hid_ref[:, c0:c0 + ft] = (a * jax.nn.sigmoid(a) * u).astype(BF16)
    o_ref[...] += _dot(hid_ref[...], wd_ref[...])


def _ffn(h, ya, yp, ym, wo, g, wgu, wd):
    N, D = h.shape
    FF = wd.shape[0]
    T = _pick_tile(N, FFN_ROW_TILE)
    assert FF % FF_TILE_DENSE == 0
    row = lambda a: pl.BlockSpec((T, a.shape[1]), lambda i: (i, 0))
    full = lambda a: pl.BlockSpec(a.shape, lambda i: (0, 0), pipeline_mode=pl.Buffered(1))
    return pl.pallas_call(
        _ffn_kernel,
        grid=(N // T,),
        in_specs=[row(h), row(ya), row(yp), row(ym), full(wo), full(g),
                  full(wgu), full(wd)],
        out_specs=row(h),
        out_shape=jax.ShapeDtypeStruct((N, D), F32),
        scratch_shapes=[pltpu.VMEM((T, D), BF16), pltpu.VMEM((T, FF), BF16)],
        compiler_params=_cparams(("parallel",)),
        name="dense_ffn",
    )(h, ya, yp, ym, wo, g, wgu, wd)


def _router_kernel(h_ref, ya_ref, yp_ref, ym_ref, wo_ref, g_ref, wrh_ref, wrl_ref, tri_ref,
                   hf_ref, route_ref, cnt_ref, run_ref):
    i = pl.program_id(0)
    T = h_ref.shape[0]

    @pl.when(i == 0)
    def _():
        run_ref[...] = jnp.zeros_like(run_ref)

    hf = h_ref[...] + _dot(jnp.concatenate([ya_ref[...], yp_ref[...], ym_ref[...]], axis=1), wo_ref[...])
    hf_ref[...] = hf
    xn = _rms(hf, g_ref[...])
    lane = lax.broadcasted_iota(jnp.int32, (T, LANE), 1).astype(F32)
    x_hi = xn.astype(BF16)
    x_lo = (xn - x_hi.astype(F32)).astype(BF16)
    logits = _dot(x_hi, wrh_ref[...]) + (_dot(x_lo, wrh_ref[...]) + _dot(x_hi, wrl_ref[...]))
    logits = jnp.where(lane < N_EXPERTS, logits, NEG_INF)
    v1 = jnp.max(logits, axis=-1, keepdims=True)
    i1 = jnp.min(jnp.where(logits == v1, lane, float(LANE)), axis=-1, keepdims=True)
    rest = jnp.where(lane == i1, NEG_INF, logits)
    v2 = jnp.max(rest, axis=-1, keepdims=True)
    i2 = jnp.min(jnp.where(rest == v2, lane, float(LANE)), axis=-1, keepdims=True)
    e2 = jnp.exp(v2 - v1)
    g1 = 1.0 / (1.0 + e2)
    g2 = e2 / (1.0 + e2)
    member = jnp.logical_or(lane == i1, lane == i2).astype(F32)
    rank = run_ref[0:1, :] + _dot(tri_ref[...], member.astype(BF16))
    r1 = jnp.sum(jnp.where(lane == i1, rank, 0.0), axis=-1, keepdims=True)
    r2 = jnp.sum(jnp.where(lane == i2, rank, 0.0), axis=-1, keepdims=True)
    run_ref[...] = run_ref[...] + jnp.sum(member, axis=0, keepdims=True)
    cnt_ref[...] = run_ref[...]
    route = jnp.where(lane == 0, i1, jnp.where(lane == 1, i2,
            jnp.where(lane == 2, r1, jnp.where(lane == 3, r2,
            jnp.where(lane == 4, g1, jnp.where(lane == 5, g2, 0.0))))))
    route_ref[...] = route


def _router(h, ya, yp, ym, wo, g, wr):
    N, D = h.shape
    T = _pick_tile(N, FLAT_TILE)
    tri = jnp.tril(jnp.ones((T, T), BF16), -1)
    wr_hi = wr.astype(BF16)
    wr_lo = (wr - wr_hi.astype(F32)).astype(BF16)
    row = lambda a: pl.BlockSpec((T, a.shape[1]), lambda i: (i, 0))
    full = lambda a: pl.BlockSpec(a.shape, lambda i: (0, 0))
    return pl.pallas_call(
        _router_kernel,
        grid=(N // T,),
        in_specs=[row(h), row(ya), row(yp), row(ym), full(wo), full(g), full(wr_hi),
                  full(wr_lo), full(tri)],
        out_specs=[pl.BlockSpec((T, D), lambda i: (i, 0)),
                   pl.BlockSpec((T, LANE), lambda i: (i, 0)),
                   pl.BlockSpec((8, LANE), lambda i: (0, 0))],
        out_shape=[jax.ShapeDtypeStruct((N, D), F32), jax.ShapeDtypeStruct((N, LANE), F32),
                   jax.ShapeDtypeStruct((8, LANE), F32)],
        scratch_shapes=[pltpu.VMEM((8, LANE), F32)],
        compiler_params=_cparams(("arbitrary",)),
        name="router",
    )(h, ya, yp, ym, wo, g, wr_hi, wr_lo, tri)


def _dispatch_kernel(seg_ref, dest_ref, h_ref, xs_ref, zero_ref, sem):
    n = h_ref.shape[0]
    zb = zero_ref.shape[0]
    n_rows = xs_ref.shape[0]

    def pad_row(r):
        return pltpu.make_async_copy(zero_ref.at[pl.ds(0, 1)], xs_ref.at[pl.ds(r, 1)], sem)

    def tail_block(b):
        start = pl.multiple_of(seg_ref[2 * N_EXPERTS - 1] + b * zb, zb)
        return pltpu.make_async_copy(zero_ref, xs_ref.at[pl.ds(start, zb)], sem)

    def for_unwritten(start):
        for e in range(N_EXPERTS):
            lax.fori_loop(seg_ref[e], seg_ref[N_EXPERTS + e],
                          lambda r, c: (pad_row(r).start() if start else pad_row(r).wait(), c)[1], 0)
        lax.fori_loop(0, (n_rows - seg_ref[2 * N_EXPERTS - 1]) // zb,
                      lambda b, c: (tail_block(b).start() if start else tail_block(b).wait(), c)[1], 0)

    @pl.when(pl.program_id(0) == 0)
    def _():
        zero_ref[...] = jnp.zeros_like(zero_ref)
        for_unwritten(True)
        for_unwritten(False)

    def copy(t, slot):
        return pltpu.make_async_copy(h_ref.at[pl.ds(t, 1)],
                                     xs_ref.at[pl.ds(dest_ref[0, 0, 2 * t + slot], 1)], sem)

    def issue(t, c):
        copy(t, 0).start()
        copy(t, 1).start()
        return c

    lax.fori_loop(0, n, issue, 0, unroll=8)

    def drain(t, c):
        copy(t, 0).wait()
        copy(t, 1).wait()
        return c

    lax.fori_loop(0, n, drain, 0, unroll=8)


def _dispatch(seg, dest3, h, n_rows):
    n_tiles = dest3.shape[0]
    T, D = dest3.shape[2] // 2, h.shape[1]
    grid_spec = pltpu.PrefetchScalarGridSpec(
        num_scalar_prefetch=1,
        grid=(n_tiles,),
        in_specs=[pl.BlockSpec((1, 1, 2 * T), lambda i, seg: (i, 0, 0), memory_space=pltpu.SMEM),
                  pl.BlockSpec((T, D), lambda i, seg: (i, 0))],
        out_specs=pl.BlockSpec(memory_space=pl.ANY),
        scratch_shapes=[pltpu.VMEM((ZERO_ROWS, D), F32), pltpu.SemaphoreType.DMA(())],
    )
    return pl.pallas_call(
        _dispatch_kernel,
        grid_spec=grid_spec,
        out_shape=jax.ShapeDtypeStruct((n_rows, D), F32),
        compiler_params=_cparams(("arbitrary",)),
        name="dispatch",
    )(seg, dest3, h)


def _expert_kernel(be_ref, x_ref, g_ref, wgu_ref, wd_ref, o_ref, xn_ref):
    r = pl.program_id(0)
    j = pl.program_id(1)
    n_active = be_ref[be_ref.shape[0] - 1]

    @pl.when(r < n_active)
    def _():
        @pl.when(j == 0)
        def _():
            xn_ref[...] = _rms(x_ref[...], g_ref[...]).astype(BF16)

        tf = wd_ref.shape[1]
        au = _dot(xn_ref[...], wgu_ref[0])
        a, u = au[:, :tf], au[:, tf:]
        y = _dot((a * jax.nn.sigmoid(a) * u).astype(BF16), wd_ref[0])

        @pl.when(j == 0)
        def _():
            o_ref[...] = y

        @pl.when(j > 0)
        def _():
            o_ref[...] += y

    @pl.when(jnp.logical_and(r >= n_active, j == 0))
    def _():
        o_ref[...] = jnp.zeros_like(o_ref)


def _experts(block_exp, xs, g, wgu, wd):
    R, D = xs.shape
    TM = MOE_TILE
    FF = wd.shape[1]
    TF = FF_TILE_EXPERT
    assert FF % TF == 0 and R % TM == 0
    grid_spec = pltpu.PrefetchScalarGridSpec(
        num_scalar_prefetch=1,
        grid=(R // TM, FF // TF),
        in_specs=[pl.BlockSpec((TM, D), lambda r, j, be: (r, 0)),
                  pl.BlockSpec(g.shape, lambda r, j, be: (0, 0)),
                  pl.BlockSpec((1, D, 2 * TF), lambda r, j, be: (be[r], 0, j)),
                  pl.BlockSpec((1, TF, D), lambda r, j, be: (be[r], j, 0))],
        out_specs=pl.BlockSpec((TM, D), lambda r, j, be: (r, 0)),
        scratch_shapes=[pltpu.VMEM((TM, D), BF16)],
    )
    return pl.pallas_call(
        _expert_kernel,
        grid_spec=grid_spec,
        out_shape=jax.ShapeDtypeStruct((R, D), F32),
        compiler_params=_cparams(("parallel", "arbitrary")),
        name="experts",
    )(block_exp, xs, g, wgu, wd)


def _combine_kernel(dest_ref, next_ref, h_ref, route_ref, ys_ref, o_ref, buf_ref, sems):
    i = pl.program_id(0)
    T = h_ref.shape[0]
    cur = i % 2

    def copy(idx_ref, b, t, slot):
        return pltpu.make_async_copy(ys_ref.at[pl.ds(idx_ref[0, 0, 2 * t + slot], 1)],
                                     buf_ref.at[b, slot, pl.ds(t, 1)], sems.at[b])

    def issue(idx_ref, b):
        def body(t, c):
            copy(idx_ref, b, t, 0).start()
            copy(idx_ref, b, t, 1).start()
            return c
        lax.fori_loop(0, T, body, 0, unroll=8)

    @pl.when(i == 0)
    def _():
        issue(dest_ref, 0)

    @pl.when(i + 1 < pl.num_programs(0))
    def _():
        issue(next_ref, 1 - cur)

    def drain(t, c):
        copy(dest_ref, cur, t, 0).wait()
        copy(dest_ref, cur, t, 1).wait()
        return c

    lax.fori_loop(0, T, drain, 0, unroll=8)
    route = route_ref[...]
    o_ref[...] = h_ref[...] + route[:, 4:5] * buf_ref[cur, 0] + route[:, 5:6] * buf_ref[cur, 1]


def _combine(dest3, h, route, ys):
    N, D = h.shape
    T = dest3.shape[2] // 2
    n_tiles = N // T
    idx_spec = lambda f: pl.BlockSpec((1, 1, 2 * T), lambda i: (f(i), 0, 0), memory_space=pltpu.SMEM)
    return pl.pallas_call(
        _combine_kernel,
        grid=(n_tiles,),
        in_specs=[idx_spec(lambda i: i), idx_spec(lambda i: jnp.minimum(i + 1, n_tiles - 1)),
                  pl.BlockSpec((T, D), lambda i: (i, 0)),
                  pl.BlockSpec((T, LANE), lambda i: (i, 0)),
                  pl.BlockSpec(memory_space=pl.ANY)],
        out_specs=pl.BlockSpec((T, D), lambda i: (i, 0)),
        out_shape=jax.ShapeDtypeStruct((N, D), F32),
        scratch_shapes=[pltpu.VMEM((2, 2, T, D), F32), pltpu.SemaphoreType.DMA((2,))],
        compiler_params=_cparams(("arbitrary",)),
        name="combine",
    )(dest3, dest3, h, route, ys)


def _moe(mixed, g, wr, wgu, wd):
    N, D = mixed[0].shape
    T = _pick_tile(N, FLAT_TILE)
    h, route, counts = _router(*mixed, g, wr)
    cnt = counts[0, :N_EXPERTS].astype(jnp.int32)
    padded = (cnt + MOE_TILE - 1) // MOE_TILE * MOE_TILE
    pend = jnp.cumsum(padded)
    pstart = pend - padded
    idx = route[:, 0:2].astype(jnp.int32)
    dest = pstart[idx] + route[:, 2:4].astype(jnp.int32)
    dest3 = dest.reshape(N // T, 1, 2 * T)
    n_blocks = -(-2 * N // MOE_TILE) + N_EXPERTS
    block_exp = jnp.minimum(jnp.searchsorted(pend, jnp.arange(n_blocks) * MOE_TILE, side='right'),
                            N_EXPERTS - 1).astype(jnp.int32)
    block_exp = jnp.concatenate([block_exp, (pend[-1:] // MOE_TILE).astype(jnp.int32)])
    seg = jnp.concatenate([pstart + cnt, pend]).astype(jnp.int32)
    xs = _dispatch(seg, dest3, h, n_blocks * MOE_TILE)
    ys = _experts(block_exp, xs, g, wgu, wd)
    return _combine(dest3, h, route, ys)


def _final_kernel(h_ref, g_ref, o_ref):
    o_ref[0] = _rms(h_ref[0, N_META:, :], g_ref[...])


def _final(h, g):
    B, L, D = h.shape
    return pl.pallas_call(
        _final_kernel,
        grid=(B,),
        in_specs=[pl.BlockSpec((1, L, D), lambda b: (b, 0, 0)), pl.BlockSpec(g.shape, lambda b: (0, 0))],
        out_specs=pl.BlockSpec((1, L - N_META, D), lambda b: (b, 0, 0)),
        out_shape=jax.ShapeDtypeStruct((B, L - N_META, D), F32),
        compiler_params=_cparams(("parallel",)),
        name="final_norm",
    )(h, g)


def _rope_tables(L):
    pos = jnp.arange(L, dtype=jnp.int32)
    inv_freq = ROPE_THETA ** (-jnp.arange(0, QK_ROPE, 2, dtype=F32) / QK_ROPE)
    ang = pos.astype(F32)[:, None] * inv_freq[None, :]
    cos, sin = jnp.cos(ang), jnp.sin(ang)
    z = lambda n: jnp.zeros((L, n), F32)
    scale = (QK_NOPE + QK_ROPE) ** -0.5 * float(np.log2(np.e))
    cq = jnp.concatenate([jnp.ones((L, QK_NOPE), F32), cos, cos, z(32)], axis=1) * scale
    s1q = jnp.concatenate([z(QK_NOPE), -sin, z(48)], axis=1) * scale
    s2q = jnp.concatenate([z(QK_NOPE + 16), sin, z(32)], axis=1) * scale
    g = 2 * MLSTM_HEADS
    ck = jnp.concatenate([z(g), cos, cos, z(96 - g)], axis=1)
    s1k = jnp.concatenate([z(g), -sin, z(112 - g)], axis=1)
    s2k = jnp.concatenate([z(g + 16), sin, z(96 - g)], axis=1)
    return jnp.concatenate([cq, s1q, s2q, ck, s1k, s2k], axis=1)


def _interleave_tiles(wg, wu, tile):
    pieces = []
    for c0 in range(0, wg.shape[-1], tile):
        pieces += [wg[..., c0:c0 + tile], wu[..., c0:c0 + tile]]
    return jnp.concatenate(pieces, axis=-1).astype(BF16)


def _layer_weights(l, w_in, w_q_up, w_kv_up, pool_w, w_out):
    pts = np.cumsum([Q_LORA, KV_LORA, QK_ROPE, POOL_WIDTH, 2 * MLSTM_WIDTH, MLSTM_WIDTH, MLSTM_WIDTH]).tolist()
    wi = w_in[l]
    w_cq, w_ckv, w_kr, w_xp, w_mqk, w_mv, w_mo, w_mg = jnp.split(wi, pts, axis=1)
    small = jnp.concatenate([w_mg, w_kr, jnp.zeros((D_MODEL, LANE - QK_ROPE - 2 * MLSTM_HEADS), F32)], axis=1)
    wbig = jnp.concatenate([w_cq, w_ckv, w_xp, w_mqk, w_mv, w_mo, small], axis=1).astype(BF16)
    wq = w_q_up[l].reshape(Q_LORA, MLA_HEADS, QK_NOPE + QK_ROPE)
    wq = jnp.pad(wq, ((0, 0), (0, 0), (0, HEAD_PAD - QK_NOPE - QK_ROPE))).reshape(Q_LORA, -1).astype(BF16)
    wkv = w_kv_up[l].reshape(KV_LORA, MLA_HEADS, QK_NOPE + V_HEAD)
    wk = jnp.pad(wkv[:, :, :QK_NOPE], ((0, 0), (0, 0), (0, HEAD_PAD - QK_NOPE))).reshape(KV_LORA, -1).astype(BF16)
    wv = wkv[:, :, QK_NOPE:].reshape(KV_LORA, -1).astype(BF16)
    pw = jax.scipy.linalg.block_diag(*[pool_w[l, g] for g in range(POOL_GROUPS)]).astype(BF16)
    return dict(wbig=wbig, wq=wq, wk=wk, wv=wv, pw=pw, wo=w_out[l].astype(BF16))


def kernel(x, meta_tokens, attn_norm, w_in, q_norm, w_q_up, kv_norm, w_kv_up, pool_w, pool_scale, conv_w, conv_b, gate_bias, mlstm_norm, w_out, ffn_norm, dense_w_gate, dense_w_up, dense_w_down, router_w, moe_w_gate, moe_w_up, moe_w_down, final_norm):
    B, S, D = x.shape
    L = N_META + S
    N = B * L
    depth = w_in.shape[0]
    n_full = L // SEQ_CHUNK
    assert S % ATTN_TILE == 0 and S % SEQ_CHUNK == 0

    meta = jnp.broadcast_to(meta_tokens[None].astype(x.dtype), (B, N_META, D))
    h = jnp.concatenate([meta, x], axis=1)
    tab = _rope_tables(L)
    row = lambda a: a.reshape(1, -1).astype(F32)

    for l in range(depth):
        lw = _layer_weights(l, w_in, w_q_up, w_kv_up, pool_w, w_out)
        q, k, v, xp, mqk, mv, mo, zs = _inproj(h, row(attn_norm[l]), lw['wbig'], row(q_norm[l]), lw['wq'],
                                               row(kv_norm[l]), lw['wk'], lw['wv'], tab)
        ya = _attention(q, k, v)
        gbc = jnp.pad(gate_bias[l], (0, LANE - 2 * MLSTM_HEADS)).reshape(1, LANE).astype(F32)
        yp, ym = _seq_mixer(xp, mqk, mv, mo, zs, lw['pw'], row(pool_scale[l]),
                            conv_w[l].astype(F32), row(conv_b[l]), gbc, row(mlstm_norm[l]))
        mixed = (h.reshape(N, D), ya.reshape(N, -1), yp.reshape(N, -1), ym.reshape(N, -1),
                 lw['wo'])
        j = l // 2
        if l % 2 == 0:
            wgu = _interleave_tiles(dense_w_gate[j], dense_w_up[j], FF_TILE_DENSE)
            hf = _ffn(*mixed, row(ffn_norm[l]), wgu, dense_w_down[j].astype(BF16))
        else:
            wr = jnp.pad(router_w[j], ((0, 0), (0, LANE - N_EXPERTS))).astype(F32)
            wgu = _interleave_tiles(moe_w_gate[j], moe_w_up[j], FF_TILE_EXPERT)
            hf = _moe(mixed, row(ffn_norm[l]), wr, wgu, moe_w_down[j].astype(BF16))
        h = hf.reshape(B, L, D)
    return _final(h, row(final_norm))
```
